```python
import math
import jax, jax.numpy as jnp
from jax import lax
import numpy as np

D_MODEL = 1024
BATCH = 16
SEQ = 4096
DEPTH = 4

HEAD_DIM = 64
N_HEADS_A = 8
DILATION_PATTERNS = ((128, 1), (512, 4), (2048, 16))
N_HEADS_B = 8
N_KV_B = 2
WINDOW_B = 128
WA = N_HEADS_A * HEAD_DIM
WB_Q = N_HEADS_B * HEAD_DIM
WB_KV = N_KV_B * HEAD_DIM
MIX_WIDTH = WA + WB_Q
IN_WIDTH = 3 * WA + WB_Q + 2 * WB_KV
ROPE_THETA = 10000.0
N_EXPERTS = 16
D_EXPERT = 1024
CAPACITY_FACTOR = 2
RMS_EPS = 1e-6
NEG_INF = -1e30

kernel_name = "hybrid_longnet_swa_sink_ec_moe_encoder"


def rmsnorm(x, g):
    xf = x.astype(jnp.float32)
    y = xf * lax.rsqrt(jnp.mean(xf * xf, axis=-1, keepdims=True) + RMS_EPS)
    return (y * g.astype(jnp.float32)).astype(x.dtype)


def rope_tables(seq_len):
    inv = 1.0 / (ROPE_THETA ** (jnp.arange(0, HEAD_DIM, 2, dtype=jnp.float32) / HEAD_DIM))
    ang = jnp.arange(seq_len, dtype=jnp.float32)[:, None] * inv[None, :]
    return jnp.cos(ang), jnp.sin(ang)


def apply_rope(t, cos, sin):
    tf = t.astype(jnp.float32)
    t1, t2 = tf[..., : HEAD_DIM // 2], tf[..., HEAD_DIM // 2:]
    out = jnp.concatenate([t1 * cos - t2 * sin, t2 * cos + t1 * sin], axis=-1)
    return out.astype(t.dtype)


def banded_attention(q, k, v, half_window, sink=None):
    w = half_window
    L, hd = q.shape[-2], q.shape[-1]
    nb = -(-L // w)
    pad = nb * w - L
    qp = jnp.pad(q, [(0, 0)] * (q.ndim - 2) + [(0, pad), (0, 0)])
    qb = qp.reshape(q.shape[:-2] + (nb, w, hd))

    def windows(t):
        tp = jnp.pad(t, [(0, 0)] * (t.ndim - 2) + [(w, pad + w), (0, 0)])
        tb = tp.reshape(t.shape[:-2] + (nb + 2, w, hd))
        return jnp.concatenate([tb[..., 0:nb, :, :], tb[..., 1:nb + 1, :, :], tb[..., 2:nb + 2, :, :]], axis=-2)

    kw, vw = windows(k), windows(v)
    s = jnp.einsum('...gbqd,...bkd->...gbqk', qb, kw, preferred_element_type=jnp.float32) * (hd ** -0.5)
    qpos = jnp.arange(nb)[:, None] * w + jnp.arange(w)[None, :]
    kpos = (jnp.arange(nb)[:, None] - 1) * w + jnp.arange(3 * w)[None, :]
    valid = ((jnp.abs(qpos[:, :, None] - kpos[:, None, :]) <= w)
             & (kpos[:, None, :] >= 0) & (kpos[:, None, :] < L))
    s = jnp.where(valid, s, NEG_INF)
    m = jnp.max(s, axis=-1)
    if sink is not None:
        sink_b = jnp.broadcast_to(sink.astype(jnp.float32), m.shape)
        m = jnp.maximum(m, sink_b)
    e = jnp.exp(s - m[..., None])
    denom = jnp.sum(e, axis=-1)
    if sink is not None:
        denom = denom + jnp.exp(sink_b - m)
    p = e / denom[..., None]
    o = jnp.einsum('...gbqk,...bkd->...gbqd', p.astype(v.dtype), vw)
    lse = m + jnp.log(denom)
    o = o.reshape(o.shape[:-4] + (o.shape[-4], nb * w, hd))[..., :L, :]
    lse = lse.reshape(lse.shape[:-2] + (nb * w,))[..., :L]
    return o, lse


def longnet_dilated_attention(q, k, v):
    B, H, S, hd = q.shape
    outs, lses = [], []
    for window, d in DILATION_PATTERNS:
        n = S // d
        qd = q.reshape(B, H, n, d, hd).swapaxes(2, 3)[:, :, :, None]
        kd = k.reshape(B, H, n, d, hd).swapaxes(2, 3)
        vd = v.reshape(B, H, n, d, hd).swapaxes(2, 3)
        o, lse = banded_attention(qd, kd, vd, (window // 2) // d)
        outs.append(o[:, :, :, 0].swapaxes(2, 3).reshape(B, H, S, hd))
        lses.append(lse[:, :, :, 0].swapaxes(2, 3).reshape(B, H, S))
    wts = jax.nn.softmax(jnp.stack(lses, axis=0), axis=0)
    out = sum(wts[i][..., None] * outs[i].astype(jnp.float32) for i in range(len(outs)))
    return out.astype(q.dtype)


def expert_choice_moe(h, w_router, w_gate, w_up, w_down):
    B, S, D = h.shape
    C = min(CAPACITY_FACTOR * S // N_EXPERTS, S)
    logits = jnp.einsum('bsd,de->bse', h, w_router, preferred_element_type=jnp.float32)
    aff = jax.nn.softmax(logits, axis=-1)
    gate, idx = lax.top_k(jnp.swapaxes(aff, 1, 2), C)
    bidx = jnp.arange(B)[:, None, None]
    xe = h[bidx, idx]
    hid = jax.nn.silu(jnp.einsum('becd,edf->becf', xe, w_gate)) * jnp.einsum('becd,edf->becf', xe, w_up)
    ye = jnp.einsum('becf,efd->becd', hid, w_down) * gate[..., None].astype(h.dtype)
    return jnp.zeros_like(h).at[bidx, idx].add(ye)


def setup_inputs(seed: int = 0) -> dict:
    key = jax.random.key(seed)
    ks = jax.random.split(key, 16)
    f32 = jnp.float32
    nrm = lambda k, shape, scale: jax.random.normal(k, shape, f32) * scale
    gain = lambda k, shape: 1.0 + 0.02 * jax.random.normal(k, shape, f32)
    return {
        "x": jax.random.normal(ks[0], (BATCH, SEQ, D_MODEL), f32),
        "w_in": nrm(ks[1], (DEPTH, D_MODEL, IN_WIDTH), D_MODEL ** -0.5),
        "w_out": nrm(ks[2], (DEPTH, MIX_WIDTH, D_MODEL), MIX_WIDTH ** -0.5),
        "g_attn": gain(ks[3], (DEPTH, D_MODEL)),
        "g_mix_a": gain(ks[4], (DEPTH, WA)),
        "g_mix_b": gain(ks[5], (DEPTH, WB_Q)),
        "sink": nrm(ks[6], (DEPTH, N_KV_B, N_HEADS_B // N_KV_B), 1.0),
        "g_ffn": gain(ks[7], (DEPTH, D_MODEL)),
        "w_router": nrm(ks[8], (DEPTH, D_MODEL, N_EXPERTS), D_MODEL ** -0.5),
        "w_gate": nrm(ks[9], (DEPTH, N_EXPERTS, D_MODEL, D_EXPERT), D_MODEL ** -0.5),
        "w_up": nrm(ks[10], (DEPTH, N_EXPERTS, D_MODEL, D_EXPERT), D_MODEL ** -0.5),
        "w_down": nrm(ks[11], (DEPTH, N_EXPERTS, D_EXPERT, D_MODEL), D_EXPERT ** -0.5),
        "g_final": gain(ks[12], (D_MODEL,)),
    }


def reference(x, w_in, w_out, g_attn, g_mix_a, g_mix_b, sink, g_ffn,
              w_router, w_gate, w_up, w_down, g_final):
    B, S, D = x.shape
    cos, sin = rope_tables(S)
    split_pts = np.cumsum([WA, WA, WA, WB_Q, WB_KV])

    def heads(t, n):
        return t.reshape(B, S, n, HEAD_DIM).transpose(0, 2, 1, 3)

    for l in range(DEPTH):
        h = rmsnorm(x, g_attn[l])
        proj = jnp.einsum('bsd,de->bse', h, w_in[l])
        qa, ka, va, qb, kb, vb = jnp.split(proj, split_pts, axis=-1)
        qa = apply_rope(heads(qa, N_HEADS_A), cos, sin)
        ka = apply_rope(heads(ka, N_HEADS_A), cos, sin)
        oa = longnet_dilated_attention(qa, ka, heads(va, N_HEADS_A))
        oa = oa.transpose(0, 2, 1, 3).reshape(B, S, WA)
        g = N_HEADS_B // N_KV_B
        qb = apply_rope(heads(qb, N_HEADS_B), cos, sin).reshape(B, N_KV_B, g, S, HEAD_DIM)
        kb = apply_rope(heads(kb, N_KV_B), cos, sin)
        ob, _ = banded_attention(qb, kb, heads(vb, N_KV_B), WINDOW_B, sink=sink[l][:, :, None, None])
        ob = ob.reshape(B, N_HEADS_B, S, HEAD_DIM).transpose(0, 2, 1, 3).reshape(B, S, WB_Q)
        mix = jnp.concatenate([rmsnorm(oa, g_mix_a[l]), rmsnorm(ob, g_mix_b[l])], axis=-1)
        x = x + jnp.einsum('bse,ed->bsd', mix, w_out[l])
        h2 = rmsnorm(x, g_ffn[l])
        x = x + expert_choice_moe(h2, w_router[l], w_gate[l], w_up[l], w_down[l])
    return rmsnorm(x, g_final)
```

```python
import functools

import jax
import jax.numpy as jnp
from jax import lax
from jax.experimental import pallas as pl
from jax.experimental.pallas import tpu as pltpu

HEAD_DIM = 64
LANES = 128
N_HEADS_A = 8
DILATIONS = ((128, 1), (512, 4), (2048, 16))
N_HEADS_B = 8
N_KV_B = 2
WINDOW_B = 128
WA = N_HEADS_A * HEAD_DIM
WB_Q = N_HEADS_B * HEAD_DIM
WB_KV = N_KV_B * HEAD_DIM
ROPE_THETA = 10000.0
N_EXPERTS = 16
CAPACITY_FACTOR = 2
RMS_EPS = 1e-6
NEG_INF = -1e30
VMEM_LIMIT = 56 * 1024 * 1024

F32 = jnp.float32
BF16 = jnp.bfloat16
_NT = (((1,), (1,)), ((), ()))


def _lower_half_mask():
    return lax.broadcasted_iota(jnp.int32, (1, LANES), 1) < HEAD_DIM


def _inproj_kernel(x_ref, g_ref, w_ref, cos_ref, sin_ref,
                   qa_ref, ka_ref, va_ref, qb_ref, kb_ref, vb_ref, h_scr):
    x = x_ref[0]
    ms = jnp.mean(x * x, axis=-1, keepdims=True)
    h_scr[...] = (x * lax.rsqrt(ms + RMS_EPS) * g_ref[...]).astype(BF16)
    cos = cos_ref[...]
    sin = sin_ref[...]
    lane = lax.broadcasted_iota(jnp.int32, (1, LANES), 1)
    first = (lane % HEAD_DIM) < (HEAD_DIM // 2)

    def rope(t, scale):
        partner = jnp.where(first, pltpu.roll(t, LANES - HEAD_DIM // 2, 1),
                            pltpu.roll(t, HEAD_DIM // 2, 1))
        return (t * cos + partner * sin) * scale

    q_scale = HEAD_DIM ** -0.5
    n_a = WA // LANES
    n_b = WB_Q // LANES
    plan = (
        (qa_ref, True, 0 * n_a, n_a, True, q_scale),
        (ka_ref, True, 1 * n_a, n_a, True, 1.0),
        (va_ref, True, 2 * n_a, n_a, False, 1.0),
        (qb_ref, False, 3 * n_a, n_b, True, q_scale),
        (kb_ref, False, 3 * n_a + n_b, 1, True, 1.0),
        (vb_ref, False, 3 * n_a + n_b + 1, 1, False, 1.0),
    )
    h = h_scr[...]
    for ref, tile_major, start, count, do_rope, scale in plan:
        for c0 in range(0, count, 2):
            width = min(2, count - c0)
            col = (start + c0) * LANES
            t = jnp.dot(h, w_ref[:, col:col + width * LANES], preferred_element_type=F32)
            for j in range(width):
                tj = t[:, j * LANES:(j + 1) * LANES]
                val = (rope(tj, scale) if do_rope else tj).astype(BF16)
                if tile_major:
                    ref[0, c0 + j] = val
                else:
                    ref[0, :, (c0 + j) * LANES:(c0 + j + 1) * LANES] = val


def _inproj(x, g, w, cos, sin, *, tm):
    B, S, D = x.shape
    n_s = S // tm
    n_a = WA // LANES
    a_shape = jax.ShapeDtypeStruct((B, n_a, S, LANES), BF16)
    a_spec = pl.BlockSpec((1, n_a, tm, LANES), lambda i: (i // n_s, 0, i % n_s, 0))

    def row_spec(width):
        return pl.BlockSpec((1, tm, width), lambda i: (i // n_s, i % n_s, 0))

    tab_spec = pl.BlockSpec((tm, LANES), lambda i: (i % n_s, 0))
    return pl.pallas_call(
        _inproj_kernel,
        grid=(B * n_s,),
        in_specs=[row_spec(D),
                  pl.BlockSpec((1, D), lambda i: (0, 0)),
                  pl.BlockSpec(w.shape, lambda i: (0, 0)),
                  tab_spec, tab_spec],
        out_specs=[a_spec, a_spec, a_spec, row_spec(WB_Q), row_spec(WB_KV), row_spec(WB_KV)],
        out_shape=[a_shape, a_shape, a_shape,
                   jax.ShapeDtypeStruct((B, S, WB_Q), BF16),
                   jax.ShapeDtypeStruct((B, S, WB_KV), BF16),
                   jax.ShapeDtypeStruct((B, S, WB_KV), BF16)],
        scratch_shapes=[pltpu.VMEM((tm, D), BF16)],
        compiler_params=pltpu.CompilerParams(dimension_semantics=("arbitrary",),
                                             vmem_limit_bytes=VMEM_LIMIT),
        name="inproj",
    )(x, g.reshape(1, D), w, cos, sin)


def _band_scores(lhs, k, off, half_window, rows_per_head):
    s = lax.dot_general(lhs, k, _NT, preferred_element_type=F32)
    row = lax.broadcasted_iota(jnp.int32, s.shape, 0) & (rows_per_head - 1)
    col = lax.broadcasted_iota(jnp.int32, s.shape, 1)
    rel = row - col + off
    return jnp.where(jnp.abs(rel) <= half_window, s, NEG_INF)


def _attn_a_kernel(q_ref, k_ref, v_ref, o_ref, qf, kf, vf, acc_s, m_s, l_s, *, S, tq):
    lo = _lower_half_mask()
    qf[...] = q_ref[0, 0].astype(F32)
    kf[...] = k_ref[0, 0].astype(F32)
    vf[...] = v_ref[0, 0].astype(F32)

    def block(q, k, v, off, half_window):
        zero = jnp.zeros_like(q)
        lhs = jnp.concatenate([jnp.where(lo, q, zero), jnp.where(lo, zero, q)], axis=0)
        s = _band_scores(lhs, k, off, half_window, tq)
        m = jnp.max(s, axis=1, keepdims=True)
        e = jnp.exp(s - m)
        l = jnp.sum(e, axis=1, keepdims=True)
        acc2 = jnp.dot(e.astype(BF16), v, preferred_element_type=F32)
        acc = jnp.where(lo, acc2[:tq], acc2[tq:])
        m_t = jnp.where(lo, m[:tq], m[tq:])
        l_t = jnp.where(lo, l[:tq], l[tq:])
        return acc, m_t, l_t

    for p_idx, (window, d) in enumerate(DILATIONS):
        L = S // d
        hw = (window // 2) // d
        kw = min(L, tq + 2 * hw)
        n_qb = L // tq
        last = p_idx == len(DILATIONS) - 1

        def body(it, carry, d=d, L=L, hw=hw, kw=kw, n_qb=n_qb, first=(p_idx == 0), last=last):
            r = it // n_qb
            q0 = (it % n_qb) * tq
            ks = jnp.clip(q0 - hw, 0, L - kw)
            if d == 1:
                rows_q = pl.ds(pl.multiple_of(q0, tq), tq)
                rows_k = pl.ds(pl.multiple_of(ks, hw), kw)
                q = q_ref[0, 0, rows_q, :]
                k = k_ref[0, 0, rows_k, :]
                v = v_ref[0, 0, rows_k, :]
            else:
                rows_q = pl.ds(r + d * q0, tq, stride=d)
                rows_k = pl.ds(r + d * ks, kw, stride=d)
                q = qf[rows_q, :].astype(BF16)
                k = kf[rows_k, :].astype(BF16)
                v = vf[rows_k, :].astype(BF16)
            acc, m_t, l_t = block(q, k, v, q0 - ks, hw)
            if not first:
                m_old = m_s[rows_q, :]
                m_new = jnp.maximum(m_old, m_t)
                a_old = jnp.exp(m_old - m_new)
                a_new = jnp.exp(m_t - m_new)
                acc = acc_s[rows_q, :] * a_old + acc * a_new
                l_t = l_s[rows_q, :] * a_old + l_t * a_new
                m_t = m_new
            if last:
                o_ref[0, 0, rows_q, :] = acc / l_t
            else:
                acc_s[rows_q, :] = acc
                m_s[rows_q, :] = m_t
                l_s[rows_q, :] = l_t
            return carry

        lax.fori_loop(0, d * n_qb, body, 0)


def _attn_a(qa, ka, va, *, tq=128):
    B, n_a, S, _ = qa.shape
    spec = pl.BlockSpec((1, 1, S, LANES), lambda b, c: (b, c, 0, 0))
    scr = pltpu.VMEM((S, LANES), F32)
    return pl.pallas_call(
        functools.partial(_attn_a_kernel, S=S, tq=tq),
        grid=(B, n_a),
        in_specs=[spec, spec, spec],
        out_specs=spec,
        out_shape=jax.ShapeDtypeStruct((B, n_a, S, LANES), F32),
        scratch_shapes=[scr] * 6,
        compiler_params=pltpu.CompilerParams(dimension_semantics=("arbitrary", "arbitrary"),
                                             vmem_limit_bytes=VMEM_LIMIT),
        name="attn_a",
    )(qa, ka, va)


def _attn_b_kernel(q_ref, k_ref, v_ref, sink_ref, o_ref, ksw, vsw, *, S, tq):
    lo = _lower_half_mask()
    hw = WINDOW_B
    kw = min(S, tq + 2 * hw)
    ksw[...] = pltpu.roll(k_ref[0].astype(F32), HEAD_DIM, 1).astype(BF16)
    vsw[...] = pltpu.roll(v_ref[0].astype(F32), HEAD_DIM, 1).astype(BF16)
    n_c = WB_Q // LANES
    straight = [(c, (2 * c) // (N_HEADS_B // N_KV_B) == 0) for c in range(n_c)]

    def sink_col(heads):
        return jnp.concatenate(
            [jnp.broadcast_to(sink_ref[h:h + 1, 0:1], (tq, 1)) for h in heads], axis=0)

    heads_x = [2 * c + (0 if is_lo else 1) for c, is_lo in straight]
    heads_y = [2 * c + (1 if is_lo else 0) for c, is_lo in straight]

    def softmax_pv(lhs, k, v, off, sink):
        s = _band_scores(lhs, k, off, hw, tq)
        m = jnp.maximum(jnp.max(s, axis=1, keepdims=True), sink)
        e = jnp.exp(s - m)
        den = jnp.sum(e, axis=1, keepdims=True) + jnp.exp(sink - m)
        return jnp.dot(e.astype(BF16), v, preferred_element_type=F32) / den

    def body(i, carry):
        q0 = pl.multiple_of(i * tq, tq)
        ks = pl.multiple_of(jnp.clip(q0 - hw, 0, S - kw), tq)
        rows_q = pl.ds(q0, tq)
        rows_k = pl.ds(ks, kw)
        qs = [q_ref[0, rows_q, c * LANES:(c + 1) * LANES] for c in range(n_c)]
        zero = jnp.zeros_like(qs[0])
        lhs_x = jnp.concatenate(
            [jnp.where(lo, qs[c], zero) if is_lo else jnp.where(lo, zero, qs[c])
             for c, is_lo in straight], axis=0)
        lhs_y = jnp.concatenate(
            [jnp.where(lo, zero, qs[c]) if is_lo else jnp.where(lo, qs[c], zero)
             for c, is_lo in straight], axis=0)
        off = q0 - ks
        ox = softmax_pv(lhs_x, k_ref[0, rows_k, :], v_ref[0, rows_k, :], off, sink_col(heads_x))
        oy = softmax_pv(lhs_y, ksw[rows_k, :], vsw[rows_k, :], off, sink_col(heads_y))
        for c, is_lo in straight:
            xs, ys = ox[c * tq:(c + 1) * tq], oy[c * tq:(c + 1) * tq]
            out = jnp.where(lo, xs, ys) if is_lo else jnp.where(lo, ys, xs)
            o_ref[0, rows_q, c * LANES:(c + 1) * LANES] = out.astype(BF16)
        return carry

    lax.fori_loop(0, S // tq, body, 0)


def _attn_b(qb, kb, vb, sink, *, tq=128):
    B, S, _ = qb.shape
    sink_tab = jnp.broadcast_to(sink.reshape(N_HEADS_B, 1).astype(F32), (N_HEADS_B, LANES))
    kv_spec = pl.BlockSpec((1, S, WB_KV), lambda b: (b, 0, 0))
    q_spec = pl.BlockSpec((1, S, WB_Q), lambda b: (b, 0, 0))
    return pl.pallas_call(
        functools.partial(_attn_b_kernel, S=S, tq=tq),
        grid=(B,),
        in_specs=[q_spec, kv_spec, kv_spec, pl.BlockSpec((N_HEADS_B, LANES), lambda b: (0, 0))],
        out_specs=q_spec,
        out_shape=jax.ShapeDtypeStruct((B, S, WB_Q), BF16),
        scratch_shapes=[pltpu.VMEM((S, WB_KV), BF16)] * 2,
        compiler_params=pltpu.CompilerParams(dimension_semantics=("arbitrary",),
                                             vmem_limit_bytes=VMEM_LIMIT),
        name="attn_b",
    )(qb, kb, vb, sink_tab)


def _outproj_kernel(oa_ref, ob_ref, x_ref, ga_ref, gb_ref, w_ref, gf_ref, wr_ref,
                    x_out_ref, h2_ref, aff_ref, mix_scr):
    n_a = WA // LANES
    a = [oa_ref[0, c] for c in range(n_a)]
    ssq = a[0] * a[0]
    for c in range(1, n_a):
        ssq = ssq + a[c] * a[c]
    rs_a = lax.rsqrt(jnp.sum(ssq, axis=-1, keepdims=True) / WA + RMS_EPS)
    for c in range(n_a):
        mix_scr[:, c * LANES:(c + 1) * LANES] = (
            a[c] * rs_a * ga_ref[:, c * LANES:(c + 1) * LANES]).astype(BF16)
    b = ob_ref[0].astype(F32)
    rs_b = lax.rsqrt(jnp.mean(b * b, axis=-1, keepdims=True) + RMS_EPS)
    mix_scr[:, WA:] = (b * rs_b * gb_ref[...]).astype(BF16)
    x = x_ref[0] + jnp.dot(mix_scr[...], w_ref[...], preferred_element_type=F32)
    x_out_ref[0] = x
    h2 = x * lax.rsqrt(jnp.mean(x * x, axis=-1, keepdims=True) + RMS_EPS) * gf_ref[...]
    h2_ref[0] = h2.astype(BF16)
    logits = lax.dot_general(wr_ref[...], h2, _NT, precision=lax.Precision.HIGHEST,
                             preferred_element_type=F32)
    z = jnp.exp(logits - jnp.max(logits, axis=0, keepdims=True))
    aff_ref[0] = z / jnp.sum(z, axis=0, keepdims=True)


def _outproj(oa, ob, x, ga, gb, w, gf, wr_t, *, tm):
    B, S, D = x.shape
    n_s = S // tm
    n_a = WA // LANES
    E = wr_t.shape[0]
    const = lambda shape: pl.BlockSpec(shape, lambda i: (0,) * len(shape))
    row = lambda width: pl.BlockSpec((1, tm, width), lambda i: (i // n_s, i % n_s, 0))
    return pl.pallas_call(
        _outproj_kernel,
        grid=(B * n_s,),
        in_specs=[pl.BlockSpec((1, n_a, tm, LANES), lambda i: (i // n_s, 0, i % n_s, 0)),
                  row(WB_Q), row(D), const((1, WA)), const((1, WB_Q)), const(w.shape),
                  const((1, D)), const(wr_t.shape)],
        out_specs=[row(D), row(D), pl.BlockSpec((1, E, tm), lambda i: (i // n_s, 0, i % n_s))],
        out_shape=[jax.ShapeDtypeStruct((B, S, D), F32),
                   jax.ShapeDtypeStruct((B, S, D), BF16),
                   jax.ShapeDtypeStruct((B, E, S), F32)],
        scratch_shapes=[pltpu.VMEM((tm, WA + WB_Q), BF16)],
        compiler_params=pltpu.CompilerParams(dimension_semantics=("arbitrary",),
                                             vmem_limit_bytes=VMEM_LIMIT),
        name="outproj",
    )(oa, ob, x, ga.reshape(1, WA), gb.reshape(1, WB_Q), w, gf.reshape(1, D), wr_t)


def _ffn_kernel(xe_ref, gate_ref, wg_ref, wu_ref, wd_ref, y_ref):
    xe = xe_ref[0, 0]
    a = jnp.dot(xe, wg_ref[0], preferred_element_type=F32)
    u = jnp.dot(xe, wu_ref[0], preferred_element_type=F32)
    hid = (a / (1.0 + jnp.exp(-a)) * u).astype(BF16)
    y = jnp.dot(hid, wd_ref[0], preferred_element_type=F32)
    y_ref[0, 0] = y * gate_ref[0, 0]


def _ffn(xe, gate, wg, wu, wd):
    B, E, C, D = xe.shape
    Fd = wg.shape[-1]
    tok = lambda width: pl.BlockSpec((1, 1, C, width), lambda e, b: (b, e, 0, 0))
    return pl.pallas_call(
        _ffn_kernel,
        grid=(E, B),
        in_specs=[tok(D), tok(1),
                  pl.BlockSpec((1, D, Fd), lambda e, b: (e, 0, 0)),
                  pl.BlockSpec((1, D, Fd), lambda e, b: (e, 0, 0)),
                  pl.BlockSpec((1, Fd, D), lambda e, b: (e, 0, 0))],
        out_specs=tok(D),
        out_shape=jax.ShapeDtypeStruct((B, E, C, D), F32),
        compiler_params=pltpu.CompilerParams(dimension_semantics=("arbitrary", "arbitrary"),
                                             vmem_limit_bytes=VMEM_LIMIT),
        name="ffn",
    )(xe, gate.reshape(B, E, C, 1), wg, wu, wd)


def _final_norm_kernel(x_ref, g_ref, o_ref):
    x = x_ref[0]
    o_ref[0] = x * lax.rsqrt(jnp.mean(x * x, axis=-1, keepdims=True) + RMS_EPS) * g_ref[...]


def _final_norm(x, g, *, tm):
    B, S, D = x.shape
    n_s = S // tm
    row = pl.BlockSpec((1, tm, D), lambda i: (i // n_s, i % n_s, 0))
    return pl.pallas_call(
        _final_norm_kernel,
        grid=(B * n_s,),
        in_specs=[row, pl.BlockSpec((1, D), lambda i: (0, 0))],
        out_specs=row,
        out_shape=jax.ShapeDtypeStruct((B, S, D), F32),
        compiler_params=pltpu.CompilerParams(dimension_semantics=("arbitrary",)),
        name="final_norm",
    )(x, g.reshape(1, D))


def _rope_tables(S):
    inv = 1.0 / (ROPE_THETA ** (jnp.arange(0, HEAD_DIM, 2, dtype=F32) / HEAD_DIM))
    ang = jnp.arange(S, dtype=F32)[:, None] * inv[None, :]
    cos, sin = jnp.cos(ang), jnp.sin(ang)
    reps = LANES // HEAD_DIM
    cos_t = jnp.tile(jnp.concatenate([cos, cos], axis=-1), (1, reps))
    sin_t = jnp.tile(jnp.concatenate([-sin, sin], axis=-1), (1, reps))
    return cos_t, sin_t


def kernel(x, w_in, w_out, g_attn, g_mix_a, g_mix_b, sink, g_ffn, w_router, w_gate, w_up, w_down, g_final):
    B, S, D = x.shape
    depth = w_in.shape[0]
    tm = min(512, S)
    C = min(CAPACITY_FACTOR * S // N_EXPERTS, S)
    cos_t, sin_t = _rope_tables(S)
    bidx = jnp.arange(B)[:, None, None]
    for l in range(depth):
        qa, ka, va, qb, kb, vb = _inproj(x, g_attn[l], w_in[l].astype(BF16), cos_t, sin_t, tm=tm)
        oa = _attn_a(qa, ka, va)
        ob = _attn_b(qb, kb, vb, sink[l])
        x, h2, aff = _outproj(oa, ob, x, g_mix_a[l], g_mix_b[l], w_out[l].astype(BF16),
                              g_ffn[l], w_router[l].T, tm=tm)
        gate, idx = lax.top_k(aff, C)
        xe = h2[bidx, idx]
        ye = _ffn(xe, gate, w_gate[l].astype(BF16), w_up[l].astype(BF16), w_down[l].astype(BF16))
        x = x.at[bidx, idx].add(ye)
    return _final_norm(x, g_final, tm=tm)
```

```python
import functools

import jax
import jax.numpy as jnp
from jax import lax
from jax.experimental import pallas as pl
from jax.experimental.pallas import tpu as pltpu

HEAD_DIM = 64
LANES = 128
N_HEADS_A = 8
DILATIONS = ((128, 1), (512, 4), (2048, 16))
N_HEADS_B = 8
N_KV_B = 2
WINDOW_B = 128
WA = N_HEADS_A * HEAD_DIM
WB_Q = N_HEADS_B * HEAD_DIM
WB_KV = N_KV_B * HEAD_DIM
ROPE_THETA = 10000.0
N_EXPERTS = 16
CAPACITY_FACTOR = 2
RMS_EPS = 1e-6
NEG_INF = -1e30
LOG2E = 1.4426950408889634
VMEM_LIMIT = 56 * 1024 * 1024

F32 = jnp.float32
BF16 = jnp.bfloat16
_NT = (((1,), (1,)), ((), ()))


def _lower_half_mask():
    return lax.broadcasted_iota(jnp.int32, (1, LANES), 1) < HEAD_DIM


def _inproj_kernel(x_ref, g_ref, w_ref, cos_ref, sin_ref,
                   qa_ref, ka_ref, va_ref, qb_ref, kb_ref, vb_ref, h_scr):
    x = x_ref[0]
    ms = jnp.mean(x * x, axis=-1, keepdims=True)
    h_scr[...] = (x * lax.rsqrt(ms + RMS_EPS) * g_ref[...]).astype(BF16)
    cos = cos_ref[...]
    sin = sin_ref[...]
    lane = lax.broadcasted_iota(jnp.int32, (1, LANES), 1)
    first = (lane % HEAD_DIM) < (HEAD_DIM // 2)

    def rope(t, scale):
        partner = jnp.where(first, pltpu.roll(t, LANES - HEAD_DIM // 2, 1),
                            pltpu.roll(t, HEAD_DIM // 2, 1))
        return (t * cos + partner * sin) * scale

    q_scale = HEAD_DIM ** -0.5 * LOG2E
    n_a = WA // LANES
    n_b = WB_Q // LANES
    plan = (
        (qa_ref, True, 0 * n_a, n_a, True, q_scale),
        (ka_ref, True, 1 * n_a, n_a, True, 1.0),
        (va_ref, True, 2 * n_a, n_a, False, 1.0),
        (qb_ref, False, 3 * n_a, n_b, True, q_scale),
        (kb_ref, False, 3 * n_a + n_b, 1, True, 1.0),
        (vb_ref, False, 3 * n_a + n_b + 1, 1, False, 1.0),
    )
    h = h_scr[...]
    for ref, tile_major, start, count, do_rope, scale in plan:
        for c0 in range(0, count, 2):
            width = min(2, count - c0)
            col = (start + c0) * LANES
            t = jnp.dot(h, w_ref[:, col:col + width * LANES], preferred_element_type=F32)
            for j in range(width):
                tj = t[:, j * LANES:(j + 1) * LANES]
                val = (rope(tj, scale) if do_rope else tj).astype(BF16)
                if tile_major:
                    ref[0, c0 + j] = val
                else:
                    ref[0, :, (c0 + j) * LANES:(c0 + j + 1) * LANES] = val


def _inproj(x, g, w, cos, sin, *, tm):
    B, S, D = x.shape
    n_s = S // tm
    n_a = WA // LANES
    a_shape = jax.ShapeDtypeStruct((B, n_a, S, LANES), BF16)
    a_spec = pl.BlockSpec((1, n_a, tm, LANES), lambda i: (i // n_s, 0, i % n_s, 0))

    def row_spec(width):
        return pl.BlockSpec((1, tm, width), lambda i: (i // n_s, i % n_s, 0))

    tab_spec = pl.BlockSpec((tm, LANES), lambda i: (i % n_s, 0))
    return pl.pallas_call(
        _inproj_kernel,
        grid=(B * n_s,),
        in_specs=[row_spec(D),
                  pl.BlockSpec((1, D), lambda i: (0, 0)),
                  pl.BlockSpec(w.shape, lambda i: (0, 0)),
                  tab_spec, tab_spec],
        out_specs=[a_spec, a_spec, a_spec, row_spec(WB_Q), row_spec(WB_KV), row_spec(WB_KV)],
        out_shape=[a_shape, a_shape, a_shape,
                   jax.ShapeDtypeStruct((B, S, WB_Q), BF16),
                   jax.ShapeDtypeStruct((B, S, WB_KV), BF16),
                   jax.ShapeDtypeStruct((B, S, WB_KV), BF16)],
        scratch_shapes=[pltpu.VMEM((tm, D), BF16)],
        compiler_params=pltpu.CompilerParams(dimension_semantics=("arbitrary",),
                                             vmem_limit_bytes=VMEM_LIMIT),
        name="inproj",
    )(x, g.reshape(1, D), w, cos, sin)


def _band_bias(tq, kw, hw):
    row = jnp.arange(2 * tq)[None, :, None] % tq
    col = jnp.arange(kw)[None, None, :]
    off = (jnp.arange(3) * hw)[:, None, None]
    return jnp.where(jnp.abs(row - col + off) <= hw, 0.0, NEG_INF).astype(F32)


def _band_window(q0, hw, L, kw):
    ks = jnp.clip(q0 - hw, 0, L - kw)
    case = jnp.where(q0 - hw < 0, 0, jnp.where(q0 - hw > L - kw, 2, 1))
    return ks, case


def _band_block(q, k, v, bias, lo):
    tq = q.shape[0]
    zero = jnp.zeros_like(q)
    lhs = jnp.concatenate([jnp.where(lo, q, zero), jnp.where(lo, zero, q)], axis=0)
    s = lax.dot_general(lhs, k, _NT, preferred_element_type=F32) + bias
    m = jnp.max(s, axis=1, keepdims=True)
    e = jnp.exp2(s - m).astype(BF16)
    v_ones = jnp.concatenate([v, jnp.ones_like(v)], axis=1)
    pv = jnp.dot(e, v_ones, preferred_element_type=F32)
    acc2, l2 = pv[:, :LANES], pv[:, LANES:]
    acc = jnp.where(lo, acc2[:tq], acc2[tq:])
    m_t = jnp.where(lo, m[:tq], m[tq:])
    l_t = jnp.where(lo, l2[:tq], l2[tq:])
    return acc, m_t, l_t


def _pattern_geometry(S, tq):
    geo = []
    for window, d in DILATIONS:
        L = S // d
        hw = (window // 2) // d
        geo.append((d, L, hw, min(L, tq + 2 * hw), L // tq))
    return geo


def _attn_a_kernel(q_ref, k_ref, v_ref, bias_ref, o_ref, tmp, qd, kd, vd, acc_s, m_s, l_s, *, S, tq):
    lo = _lower_half_mask()
    geo = _pattern_geometry(S, tq)
    for src, dst in ((q_ref, qd), (k_ref, kd), (v_ref, vd)):
        tmp[...] = src[0, 0].astype(F32)
        for p, (d, L, _, _, _) in enumerate(geo):
            if d == 1:
                continue

            def regroup(r, carry, dst=dst, p=p, d=d, L=L):
                dst[p - 1, pl.ds(pl.multiple_of(r * L, L), L), :] = (
                    tmp[pl.ds(r, L, stride=d), :].astype(BF16))
                return carry

            lax.fori_loop(0, d, regroup, 0)

    def body(j, carry):
        for p, (d, L, hw, kw, n_qb) in enumerate(geo):
            r = j // n_qb
            q0 = (j % n_qb) * tq
            ks, case = _band_window(q0, hw, L, kw)
            if d == 1:
                rows_q = pl.ds(pl.multiple_of(q0, tq), tq)
                rows_k = pl.ds(pl.multiple_of(ks, hw), kw)
                q, k, v = q_ref[0, 0, rows_q, :], k_ref[0, 0, rows_k, :], v_ref[0, 0, rows_k, :]
            else:
                rows_q = pl.ds(r + d * q0, tq, stride=d)
                dense_q = pl.ds(pl.multiple_of(r * L + q0, tq), tq)
                dense_k = pl.ds(pl.multiple_of(r * L + ks, hw), kw)
                q, k, v = qd[p - 1, dense_q, :], kd[p - 1, dense_k, :], vd[p - 1, dense_k, :]
            acc, m_t, l_t = _band_block(q, k, v, bias_ref[case], lo)
            acc_s[p, rows_q, :] = acc
            m_s[p, rows_q, :] = m_t
            l_s[p, rows_q, :] = l_t
        return carry

    lax.fori_loop(0, S // tq, body, 0, unroll=2)

    def merge(j, carry):
        rows = pl.ds(pl.multiple_of(j * tq, tq), tq)
        ms = [m_s[p, rows, :] for p in range(len(geo))]
        m = functools.reduce(jnp.maximum, ms)
        num = den = None
        for p, mp in enumerate(ms):
            w = jnp.exp2(mp - m)
            num = acc_s[p, rows, :] * w if num is None else num + acc_s[p, rows, :] * w
            den = l_s[p, rows, :] * w if den is None else den + l_s[p, rows, :] * w
        o_ref[0, 0, rows, :] = (num / den).astype(BF16)
        return carry

    lax.fori_loop(0, S // tq, merge, 0)


def _attn_a(qa, ka, va, *, tq=128):
    B, n_a, S, _ = qa.shape
    geo = _pattern_geometry(S, tq)
    _, _, hw, kw, _ = geo[0]
    assert all(g[2] == hw and g[3] == kw and g[1] % tq == 0 for g in geo)
    bias = _band_bias(tq, kw, hw)
    spec = pl.BlockSpec((1, 1, S, LANES), lambda b, c: (b, c, 0, 0))
    assert geo[0][0] == 1
    seq = pltpu.VMEM((S, LANES), F32)
    dilated = pltpu.VMEM((len(geo) - 1, S, LANES), BF16)
    state = pltpu.VMEM((len(geo), S, LANES), F32)
    return pl.pallas_call(
        functools.partial(_attn_a_kernel, S=S, tq=tq),
        grid=(B, n_a),
        in_specs=[spec, spec, spec, pl.BlockSpec(bias.shape, lambda b, c: (0, 0, 0))],
        out_specs=spec,
        out_shape=jax.ShapeDtypeStruct((B, n_a, S, LANES), BF16),
        scratch_shapes=[seq, dilated, dilated, dilated, state, state, state],
        compiler_params=pltpu.CompilerParams(dimension_semantics=("arbitrary", "arbitrary"),
                                             vmem_limit_bytes=VMEM_LIMIT),
        name="attn_a",
    )(qa, ka, va, bias)


def _attn_b_kernel(q_ref, k_ref, v_ref, sink_ref, bias_ref, o_ref, kk, vv, *, S, tq):
    lo = _lower_half_mask()
    hw = WINDOW_B
    kw = min(S, tq + 2 * hw)
    for src, dst in ((k_ref, kk), (v_ref, vv)):
        t = src[0].astype(F32)
        sw = pltpu.roll(t, HEAD_DIM, 1)
        dst[0] = jnp.where(lo, t, sw).astype(BF16)
        dst[1] = jnp.where(lo, sw, t).astype(BF16)
    n_c = WB_Q // LANES
    tiles_per_kv = n_c // N_KV_B

    def body(i, carry):
        q0 = pl.multiple_of(i * tq, tq)
        ks, case = _band_window(q0, hw, S, kw)
        rows_q = pl.ds(q0, tq)
        rows_k = pl.ds(pl.multiple_of(ks, tq), kw)
        bias = bias_ref[case]
        for c in range(n_c):
            cols = slice(c * LANES, (c + 1) * LANES)
            j = c // tiles_per_kv
            acc, m_t, l_t = _band_block(q_ref[0, rows_q, cols], kk[j, rows_k, :], vv[j, rows_k, :],
                                        bias, lo)
            sk = sink_ref[c:c + 1, :]
            m2 = jnp.maximum(m_t, sk)
            a = jnp.exp2(m_t - m2)
            den = l_t * a + jnp.exp2(sk - m2)
            o_ref[0, rows_q, cols] = (acc * a / den).astype(BF16)
        return carry

    lax.fori_loop(0, S // tq, body, 0)


def _attn_b(qb, kb, vb, sink, *, tq=128):
    B, S, _ = qb.shape
    n_c = WB_Q // LANES
    kw = min(S, tq + 2 * WINDOW_B)
    assert S % tq == 0 and tq == WINDOW_B
    bias = _band_bias(tq, kw, WINDOW_B)
    sink_tab = jnp.repeat(sink.reshape(n_c, LANES // HEAD_DIM).astype(F32) * LOG2E, HEAD_DIM, axis=1)
    kv_spec = pl.BlockSpec((1, S, WB_KV), lambda b: (b, 0, 0))
    q_spec = pl.BlockSpec((1, S, WB_Q), lambda b: (b, 0, 0))
    return pl.pallas_call(
        functools.partial(_attn_b_kernel, S=S, tq=tq),
        grid=(B,),
        in_specs=[q_spec, kv_spec, kv_spec, pl.BlockSpec((n_c, LANES), lambda b: (0, 0)),
                  pl.BlockSpec(bias.shape, lambda b: (0, 0, 0))],
        out_specs=q_spec,
        out_shape=jax.ShapeDtypeStruct((B, S, WB_Q), BF16),
        scratch_shapes=[pltpu.VMEM((N_KV_B, S, LANES), BF16)] * 2,
        compiler_params=pltpu.CompilerParams(dimension_semantics=("arbitrary",),
                                             vmem_limit_bytes=VMEM_LIMIT),
        name="attn_b",
    )(qb, kb, vb, sink_tab, bias)


def _outproj_kernel(oa_ref, ob_ref, x_ref, ga_ref, gb_ref, w_ref, gf_ref, wr_ref,
                    x_out_ref, h2_ref, aff_ref, mix_scr):
    n_a = WA // LANES
    a = [oa_ref[0, c].astype(F32) for c in range(n_a)]
    ssq = a[0] * a[0]
    for c in range(1, n_a):
        ssq = ssq + a[c] * a[c]
    rs_a = lax.rsqrt(jnp.sum(ssq, axis=-1, keepdims=True) / WA + RMS_EPS)
    for c in range(n_a):
        mix_scr[:, c * LANES:(c + 1) * LANES] = (
            a[c] * rs_a * ga_ref[:, c * LANES:(c + 1) * LANES]).astype(BF16)
    b = ob_ref[0].astype(F32)
    rs_b = lax.rsqrt(jnp.mean(b * b, axis=-1, keepdims=True) + RMS_EPS)
    mix_scr[:, WA:] = (b * rs_b * gb_ref[...]).astype(BF16)
    x = x_ref[0] + jnp.dot(mix_scr[...], w_ref[...], preferred_element_type=F32)
    x_out_ref[0] = x
    h2 = x * lax.rsqrt(jnp.mean(x * x, axis=-1, keepdims=True) + RMS_EPS) * gf_ref[...]
    h2_ref[0] = h2.astype(BF16)
    logits = lax.dot_general(wr_ref[...], h2, _NT, precision=lax.Precision.HIGHEST,
                             preferred_element_type=F32)
    z = jnp.exp(logits - jnp.max(logits, axis=0, keepdims=True))
    aff_ref[0] = z / jnp.sum(z, axis=0, keepdims=True)


def _outproj(oa, ob, x, ga, gb, w, gf, wr_t, *, tm):
    B, S, D = x.shape
    n_s = S // tm
    n_a = WA // LANES
    E = wr_t.shape[0]
    const = lambda shape: pl.BlockSpec(shape, lambda i: (0,) * len(shape))
    row = lambda width: pl.BlockSpec((1, tm, width), lambda i: (i // n_s, i % n_s, 0))
    return pl.pallas_call(
        _outproj_kernel,
        grid=(B * n_s,),
        in_specs=[pl.BlockSpec((1, n_a, tm, LANES), lambda i: (i // n_s, 0, i % n_s, 0)),
                  row(WB_Q), row(D), const((1, WA)), const((1, WB_Q)), const(w.shape),
                  const((1, D)), const(wr_t.shape)],
        out_specs=[row(D), row(D), pl.BlockSpec((1, E, tm), lambda i: (i // n_s, 0, i % n_s))],
        out_shape=[jax.ShapeDtypeStruct((B, S, D), F32),
                   jax.ShapeDtypeStruct((B, S, D), BF16),
                   jax.ShapeDtypeStruct((B, E, S), F32)],
        scratch_shapes=[pltpu.VMEM((tm, WA + WB_Q), BF16)],
        compiler_params=pltpu.CompilerParams(dimension_semantics=("arbitrary",),
                                             vmem_limit_bytes=VMEM_LIMIT),
        name="outproj",
    )(oa, ob, x, ga.reshape(1, WA), gb.reshape(1, WB_Q), w, gf.reshape(1, D), wr_t)


def _ffn_kernel(xe_ref, gate_ref, wg_ref, wu_ref, wd_ref, y_ref):
    xe = xe_ref[0, 0]
    a = jnp.dot(xe, wg_ref[0], preferred_element_type=F32)
    u = jnp.dot(xe, wu_ref[0], preferred_element_type=F32)
    hid = (a / (1.0 + jnp.exp(-a)) * u).astype(BF16)
    y = jnp.dot(hid, wd_ref[0], preferred_element_type=F32)
    y_ref[0, 0] = y * gate_ref[0, 0]


def _ffn(xe, gate, wg, wu, wd):
    B, E, C, D = xe.shape
    Fd = wg.shape[-1]
    tok = lambda width: pl.BlockSpec((1, 1, C, width), lambda e, b: (b, e, 0, 0))
    return pl.pallas_call(
        _ffn_kernel,
        grid=(E, B),
        in_specs=[tok(D), tok(1),
                  pl.BlockSpec((1, D, Fd), lambda e, b: (e, 0, 0)),
                  pl.BlockSpec((1, D, Fd), lambda e, b: (e, 0, 0)),
                  pl.BlockSpec((1, Fd, D), lambda e, b: (e, 0, 0))],
        out_specs=tok(D),
        out_shape=jax.ShapeDtypeStruct((B, E, C, D), F32),
        compiler_params=pltpu.CompilerParams(dimension_semantics=("arbitrary", "arbitrary"),
                                             vmem_limit_bytes=VMEM_LIMIT),
        name="ffn",
    )(xe, gate.reshape(B, E, C, 1), wg, wu, wd)


def _final_norm_kernel(x_ref, g_ref, o_ref):
    x = x_ref[0]
    o_ref[0] = x * lax.rsqrt(jnp.mean(x * x, axis=-1, keepdims=True) + RMS_EPS) * g_ref[...]


def _final_norm(x, g, *, tm):
    B, S, D = x.shape
    n_s = S // tm
    row = pl.BlockSpec((1, tm, D), lambda i: (i // n_s, i % n_s, 0))
    return pl.pallas_call(
        _final_norm_kernel,
        grid=(B * n_s,),
        in_specs=[row, pl.BlockSpec((1, D), lambda i: (0, 0))],
        out_specs=row,
        out_shape=jax.ShapeDtypeStruct((B, S, D), F32),
        compiler_params=pltpu.CompilerParams(dimension_semantics=("arbitrary",)),
        name="final_norm",
    )(x, g.reshape(1, D))


def _rope_tables(S):
    inv = 1.0 / (ROPE_THETA ** (jnp.arange(0, HEAD_DIM, 2, dtype=F32) / HEAD_DIM))
    ang = jnp.arange(S, dtype=F32)[:, None] * inv[None, :]
    cos, sin = jnp.cos(ang), jnp.sin(ang)
    reps = LANES // HEAD_DIM
    cos_t = jnp.tile(jnp.concatenate([cos, cos], axis=-1), (1, reps))
    sin_t = jnp.tile(jnp.concatenate([-sin, sin], axis=-1), (1, reps))
    return cos_t, sin_t


def kernel(x, w_in, w_out, g_attn, g_mix_a, g_mix_b, sink, g_ffn, w_router, w_gate, w_up, w_down, g_final):
    B, S, D = x.shape
    depth = w_in.shape[0]
    tm = min(512, S)
    C = min(CAPACITY_FACTOR * S // N_EXPERTS, S)
    cos_t, sin_t = _rope_tables(S)
    bidx = jnp.arange(B)[:, None, None]
    for l in range(depth):
        qa, ka, va, qb, kb, vb = _inproj(x, g_attn[l], w_in[l].astype(BF16), cos_t, sin_t, tm=tm)
        oa = _attn_a(qa, ka, va)
        ob = _attn_b(qb, kb, vb, sink[l])
        x, h2, aff = _outproj(oa, ob, x, g_mix_a[l], g_mix_b[l], w_out[l].astype(BF16),
                              g_ffn[l], w_router[l].T, tm=tm)
        gate, idx = lax.top_k(aff, C)
        xe = h2[bidx, idx]
        ye = _ffn(xe, gate, w_gate[l].astype(BF16), w_up[l].astype(BF16), w_down[l].astype(BF16))
        x = x.at[bidx, idx].add(ye)
    return _final_norm(x, g_final, tm=tm)
```

```python
import functools

import jax
import jax.numpy as jnp
from jax import lax
from jax.experimental import pallas as pl
from jax.experimental.pallas import tpu as pltpu

HEAD_DIM = 64
LANES = 128
N_HEADS_A = 8
DILATIONS = ((128, 1), (512, 4), (2048, 16))
N_HEADS_B = 8
N_KV_B = 2
WINDOW_B = 128
WA = N_HEADS_A * HEAD_DIM
WB_Q = N_HEADS_B * HEAD_DIM
WB_KV = N_KV_B * HEAD_DIM
ROPE_THETA = 10000.0
N_EXPERTS = 16
CAPACITY_FACTOR = 2
RMS_EPS = 1e-6
NEG_INF = -1e30
LOG2E = 1.4426950408889634
VMEM_LIMIT = 56 * 1024 * 1024
MOE_VMEM_LIMIT = 60 * 1024 * 1024

F32 = jnp.float32
BF16 = jnp.bfloat16
_NT = (((1,), (1,)), ((), ()))


def _row_pieces(x_ref, tm, n, token_tiled):
    if token_tiled:
        return [x_ref[0, pl.ds(j, tm, stride=n), :] for j in range(n)]
    return [x_ref[0, :, j * LANES:(j + 1) * LANES] for j in range(n)]


def _inv_rms(pieces, width):
    sq = pieces[0] * pieces[0]
    for p in pieces[1:]:
        sq = sq + p * p
    return lax.rsqrt(jnp.sum(sq, axis=-1, keepdims=True) / width + RMS_EPS)


def _lower_half_mask():
    return lax.broadcasted_iota(jnp.int32, (1, LANES), 1) < HEAD_DIM


def _inproj_kernel(x_ref, g_ref, w_ref, cos_ref, sin_ref,
                   qa_ref, ka_ref, va_ref, qb_ref, kb_ref, vb_ref, h_scr, *, token_tiled):
    tm, D = h_scr.shape
    xs = _row_pieces(x_ref, tm, D // LANES, token_tiled)
    rs = _inv_rms(xs, D)
    for j, xj in enumerate(xs):
        cols = slice(j * LANES, (j + 1) * LANES)
        h_scr[:, cols] = (xj * rs * g_ref[:, cols]).astype(BF16)
    cos = cos_ref[...]
    sin = sin_ref[...]
    lane = lax.broadcasted_iota(jnp.int32, (1, LANES), 1)
    first = (lane % HEAD_DIM) < (HEAD_DIM // 2)

    def rope(t, scale):
        partner = jnp.where(first, pltpu.roll(t, LANES - HEAD_DIM // 2, 1),
                            pltpu.roll(t, HEAD_DIM // 2, 1))
        return (t * cos + partner * sin) * scale

    q_scale = HEAD_DIM ** -0.5 * LOG2E
    n_a = WA // LANES
    n_b = WB_Q // LANES
    plan = (
        (qa_ref, True, 0 * n_a, n_a, True, q_scale),
        (ka_ref, True, 1 * n_a, n_a, True, 1.0),
        (va_ref, True, 2 * n_a, n_a, False, 1.0),
        (qb_ref, False, 3 * n_a, n_b, True, q_scale),
        (kb_ref, False, 3 * n_a + n_b, 1, True, 1.0),
        (vb_ref, False, 3 * n_a + n_b + 1, 1, False, 1.0),
    )
    h = h_scr[...]
    for ref, tile_major, start, count, do_rope, scale in plan:
        for c0 in range(0, count, 2):
            width = min(2, count - c0)
            col = (start + c0) * LANES
            t = jnp.dot(h, w_ref[:, col:col + width * LANES], preferred_element_type=F32)
            for j in range(width):
                tj = t[:, j * LANES:(j + 1) * LANES]
                val = (rope(tj, scale) if do_rope else tj).astype(BF16)
                if tile_major:
                    ref[0, c0 + j] = val
                else:
                    ref[0, :, (c0 + j) * LANES:(c0 + j + 1) * LANES] = val


def _inproj(x, g, w, cos, sin, *, S, tm, token_tiled):
    B = x.shape[0]
    D = w.shape[0]
    n_s = S // tm
    n_a = WA // LANES
    a_shape = jax.ShapeDtypeStruct((B, n_a, S, LANES), BF16)
    a_spec = pl.BlockSpec((1, n_a, tm, LANES), lambda i: (i // n_s, 0, i % n_s, 0))

    def row_spec(width):
        return pl.BlockSpec((1, tm, width), lambda i: (i // n_s, i % n_s, 0))

    tab_spec = pl.BlockSpec((tm, LANES), lambda i: (i % n_s, 0))
    x_spec = (pl.BlockSpec((1, tm * D // LANES, LANES), lambda i: (i // n_s, i % n_s, 0))
              if token_tiled else row_spec(D))
    return pl.pallas_call(
        functools.partial(_inproj_kernel, token_tiled=token_tiled),
        grid=(B * n_s,),
        in_specs=[x_spec,
                  pl.BlockSpec((1, D), lambda i: (0, 0)),
                  pl.BlockSpec(w.shape, lambda i: (0, 0)),
                  tab_spec, tab_spec],
        out_specs=[a_spec, a_spec, a_spec, row_spec(WB_Q), row_spec(WB_KV), row_spec(WB_KV)],
        out_shape=[a_shape, a_shape, a_shape,
                   jax.ShapeDtypeStruct((B, S, WB_Q), BF16),
                   jax.ShapeDtypeStruct((B, S, WB_KV), BF16),
                   jax.ShapeDtypeStruct((B, S, WB_KV), BF16)],
        scratch_shapes=[pltpu.VMEM((tm, D), BF16)],
        compiler_params=pltpu.CompilerParams(dimension_semantics=("arbitrary",),
                                             vmem_limit_bytes=VMEM_LIMIT),
        name="inproj",
    )(x, g.reshape(1, D), w, cos, sin)


def _band_bias(tq, kw, hw):
    row = jnp.arange(2 * tq)[None, :, None] % tq
    col = jnp.arange(kw)[None, None, :]
    off = (jnp.arange(3) * hw)[:, None, None]
    return jnp.where(jnp.abs(row - col + off) <= hw, 0.0, NEG_INF).astype(F32)


def _band_window(q0, hw, L, kw):
    ks = jnp.clip(q0 - hw, 0, L - kw)
    case = jnp.where(q0 - hw < 0, 0, jnp.where(q0 - hw > L - kw, 2, 1))
    return ks, case


def _band_block(q, k, v, bias, lo):
    tq = q.shape[0]
    zero = jnp.zeros_like(q)
    lhs = jnp.concatenate([jnp.where(lo, q, zero), jnp.where(lo, zero, q)], axis=0)
    s = lax.dot_general(lhs, k, _NT, preferred_element_type=F32) + bias
    m = jnp.max(s, axis=1, keepdims=True)
    e = jnp.exp2(s - m).astype(BF16)
    v_ones = jnp.concatenate([v, jnp.ones_like(v)], axis=1)
    pv = jnp.dot(e, v_ones, preferred_element_type=F32)
    acc2, l2 = pv[:, :LANES], pv[:, LANES:]
    acc = jnp.where(lo, acc2[:tq], acc2[tq:])
    m_t = jnp.where(lo, m[:tq], m[tq:])
    l_t = jnp.where(lo, l2[:tq], l2[tq:])
    return acc, m_t, l_t


def _pattern_geometry(S, tq):
    geo = []
    for window, d in DILATIONS:
        L = S // d
        hw = (window // 2) // d
        geo.append((d, L, hw, min(L, tq + 2 * hw), L // tq))
    return geo


def _attn_a_kernel(q_ref, k_ref, v_ref, bias_ref, o_ref, tmp, qd, kd, vd, acc_s, m_s, l_s, *, S, tq):
    lo = _lower_half_mask()
    geo = _pattern_geometry(S, tq)
    for src, dst in ((q_ref, qd), (k_ref, kd), (v_ref, vd)):
        tmp[...] = src[0, 0].astype(F32)
        for p, (d, L, _, _, _) in enumerate(geo):
            if d == 1:
                continue

            def regroup(r, carry, dst=dst, p=p, d=d, L=L):
                dst[p - 1, pl.ds(pl.multiple_of(r * L, L), L), :] = (
                    tmp[pl.ds(r, L, stride=d), :].astype(BF16))
                return carry

            lax.fori_loop(0, d, regroup, 0)

    def body(j, carry):
        for p, (d, L, hw, kw, n_qb) in enumerate(geo):
            r = j // n_qb
            q0 = (j % n_qb) * tq
            ks, case = _band_window(q0, hw, L, kw)
            if d == 1:
                rows_q = pl.ds(pl.multiple_of(q0, tq), tq)
                rows_k = pl.ds(pl.multiple_of(ks, hw), kw)
                q, k, v = q_ref[0, 0, rows_q, :], k_ref[0, 0, rows_k, :], v_ref[0, 0, rows_k, :]
            else:
                rows_q = pl.ds(r + d * q0, tq, stride=d)
                dense_q = pl.ds(pl.multiple_of(r * L + q0, tq), tq)
                dense_k = pl.ds(pl.multiple_of(r * L + ks, hw), kw)
                q, k, v = qd[p - 1, dense_q, :], kd[p - 1, dense_k, :], vd[p - 1, dense_k, :]
            acc, m_t, l_t = _band_block(q, k, v, bias_ref[case], lo)
            acc_s[p, rows_q, :] = acc
            m_s[p, rows_q, :] = m_t
            l_s[p, rows_q, :] = l_t
        return carry

    lax.fori_loop(0, S // tq, body, 0, unroll=2)

    def merge(j, carry):
        rows = pl.ds(pl.multiple_of(j * tq, tq), tq)
        ms = [m_s[p, rows, :] for p in range(len(geo))]
        m = functools.reduce(jnp.maximum, ms)
        num = den = None
        for p, mp in enumerate(ms):
            w = jnp.exp2(mp - m)
            num = acc_s[p, rows, :] * w if num is None else num + acc_s[p, rows, :] * w
            den = l_s[p, rows, :] * w if den is None else den + l_s[p, rows, :] * w
        o_ref[0, 0, rows, :] = (num / den).astype(BF16)
        return carry

    lax.fori_loop(0, S // tq, merge, 0)


def _attn_a(qa, ka, va, *, tq=128):
    B, n_a, S, _ = qa.shape
    geo = _pattern_geometry(S, tq)
    _, _, hw, kw, _ = geo[0]
    assert all(g[2] == hw and g[3] == kw and g[1] % tq == 0 for g in geo)
    bias = _band_bias(tq, kw, hw)
    spec = pl.BlockSpec((1, 1, S, LANES), lambda b, c: (b, c, 0, 0))
    assert geo[0][0] == 1
    seq = pltpu.VMEM((S, LANES), F32)
    dilated = pltpu.VMEM((len(geo) - 1, S, LANES), BF16)
    state = pltpu.VMEM((len(geo), S, LANES), F32)
    return pl.pallas_call(
        functools.partial(_attn_a_kernel, S=S, tq=tq),
        grid=(B, n_a),
        in_specs=[spec, spec, spec, pl.BlockSpec(bias.shape, lambda b, c: (0, 0, 0))],
        out_specs=spec,
        out_shape=jax.ShapeDtypeStruct((B, n_a, S, LANES), BF16),
        scratch_shapes=[seq, dilated, dilated, dilated, state, state, state],
        compiler_params=pltpu.CompilerParams(dimension_semantics=("arbitrary", "arbitrary"),
                                             vmem_limit_bytes=VMEM_LIMIT),
        name="attn_a",
    )(qa, ka, va, bias)


def _attn_b_kernel(q_ref, k_ref, v_ref, sink_ref, bias_ref, o_ref, kk, vv, *, S, tq):
    lo = _lower_half_mask()
    hw = WINDOW_B
    kw = min(S, tq + 2 * hw)
    for src, dst in ((k_ref, kk), (v_ref, vv)):
        t = src[0].astype(F32)
        sw = pltpu.roll(t, HEAD_DIM, 1)
        dst[0] = jnp.where(lo, t, sw).astype(BF16)
        dst[1] = jnp.where(lo, sw, t).astype(BF16)
    n_c = WB_Q // LANES
    tiles_per_kv = n_c // N_KV_B

    def body(i, carry):
        q0 = pl.multiple_of(i * tq, tq)
        ks, case = _band_window(q0, hw, S, kw)
        rows_q = pl.ds(q0, tq)
        rows_k = pl.ds(pl.multiple_of(ks, tq), kw)
        bias = bias_ref[case]
        for c in range(n_c):
            cols = slice(c * LANES, (c + 1) * LANES)
            j = c // tiles_per_kv
            acc, m_t, l_t = _band_block(q_ref[0, rows_q, cols], kk[j, rows_k, :], vv[j, rows_k, :],
                                        bias, lo)
            sk = sink_ref[c:c + 1, :]
            m2 = jnp.maximum(m_t, sk)
            a = jnp.exp2(m_t - m2)
            den = l_t * a + jnp.exp2(sk - m2)
            o_ref[0, rows_q, cols] = (acc * a / den).astype(BF16)
        return carry

    lax.fori_loop(0, S // tq, body, 0)


def _attn_b(qb, kb, vb, sink, *, tq=128):
    B, S, _ = qb.shape
    n_c = WB_Q // LANES
    kw = min(S, tq + 2 * WINDOW_B)
    assert S % tq == 0 and tq == WINDOW_B
    bias = _band_bias(tq, kw, WINDOW_B)
    sink_tab = jnp.repeat(sink.reshape(n_c, LANES // HEAD_DIM).astype(F32) * LOG2E, HEAD_DIM, axis=1)
    kv_spec = pl.BlockSpec((1, S, WB_KV), lambda b: (b, 0, 0))
    q_spec = pl.BlockSpec((1, S, WB_Q), lambda b: (b, 0, 0))
    return pl.pallas_call(
        functools.partial(_attn_b_kernel, S=S, tq=tq),
        grid=(B,),
        in_specs=[q_spec, kv_spec, kv_spec, pl.BlockSpec((n_c, LANES), lambda b: (0, 0)),
                  pl.BlockSpec(bias.shape, lambda b: (0, 0, 0))],
        out_specs=q_spec,
        out_shape=jax.ShapeDtypeStruct((B, S, WB_Q), BF16),
        scratch_shapes=[pltpu.VMEM((N_KV_B, S, LANES), BF16)] * 2,
        compiler_params=pltpu.CompilerParams(dimension_semantics=("arbitrary",),
                                             vmem_limit_bytes=VMEM_LIMIT),
        name="attn_b",
    )(qb, kb, vb, sink_tab, bias)


def _outproj_kernel(oa_ref, ob_ref, x_ref, ga_ref, gb_ref, w_ref, gf_ref, wr_ref,
                    x_out_ref, aff_ref, mix_scr, *, token_tiled):
    tm, D = mix_scr.shape[0], w_ref.shape[1]
    n_d = D // LANES
    n_a = WA // LANES
    a = [oa_ref[0, c].astype(F32) for c in range(n_a)]
    ssq = a[0] * a[0]
    for c in range(1, n_a):
        ssq = ssq + a[c] * a[c]
    rs_a = lax.rsqrt(jnp.sum(ssq, axis=-1, keepdims=True) / WA + RMS_EPS)
    for c in range(n_a):
        mix_scr[:, c * LANES:(c + 1) * LANES] = (
            a[c] * rs_a * ga_ref[:, c * LANES:(c + 1) * LANES]).astype(BF16)
    b = ob_ref[0].astype(F32)
    rs_b = lax.rsqrt(jnp.mean(b * b, axis=-1, keepdims=True) + RMS_EPS)
    mix_scr[:, WA:] = (b * rs_b * gb_ref[...]).astype(BF16)
    y = jnp.dot(mix_scr[...], w_ref[...], preferred_element_type=F32)
    xs = [xj + y[:, j * LANES:(j + 1) * LANES]
          for j, xj in enumerate(_row_pieces(x_ref, tm, n_d, token_tiled))]
    for j, xj in enumerate(xs):
        x_out_ref[0, pl.ds(j, tm, stride=n_d), :] = xj
    rs = _inv_rms(xs, D)
    h2 = jnp.concatenate([xj * rs * gf_ref[:, j * LANES:(j + 1) * LANES]
                          for j, xj in enumerate(xs)], axis=1)
    logits = lax.dot_general(wr_ref[...], h2, _NT, precision=lax.Precision.HIGHEST,
                             preferred_element_type=F32)
    z = jnp.exp(logits - jnp.max(logits, axis=0, keepdims=True))
    aff_ref[0] = z / jnp.sum(z, axis=0, keepdims=True)


def _outproj(oa, ob, x, ga, gb, w, gf, wr_t, *, S, tm, token_tiled):
    B = x.shape[0]
    D = w.shape[1]
    n_s = S // tm
    n_a = WA // LANES
    E = wr_t.shape[0]
    const = lambda shape: pl.BlockSpec(shape, lambda i: (0,) * len(shape))
    row = lambda width: pl.BlockSpec((1, tm, width), lambda i: (i // n_s, i % n_s, 0))
    tiled = pl.BlockSpec((1, tm * D // LANES, LANES), lambda i: (i // n_s, i % n_s, 0))
    return pl.pallas_call(
        functools.partial(_outproj_kernel, token_tiled=token_tiled),
        grid=(B * n_s,),
        in_specs=[pl.BlockSpec((1, n_a, tm, LANES), lambda i: (i // n_s, 0, i % n_s, 0)),
                  row(WB_Q), tiled if token_tiled else row(D), const((1, WA)), const((1, WB_Q)), const(w.shape),
                  const((1, D)), const(wr_t.shape)],
        out_specs=[tiled, pl.BlockSpec((1, E, tm), lambda i: (i // n_s, 0, i % n_s))],
        out_shape=[jax.ShapeDtypeStruct((B, S * D // LANES, LANES), F32),
                   jax.ShapeDtypeStruct((B, E, S), F32)],
        scratch_shapes=[pltpu.VMEM((tm, WA + WB_Q), BF16)],
        compiler_params=pltpu.CompilerParams(dimension_semantics=("arbitrary",),
                                             vmem_limit_bytes=VMEM_LIMIT),
        name="outproj",
    )(oa, ob, x, ga.reshape(1, WA), gb.reshape(1, WB_Q), w, gf.reshape(1, D), wr_t)


GROUP = 8


def _moe_kernel(idx_ref, gate_ref, x_hbm, gf_ref, wg_ref, wu_ref, wd_ref, out_hbm,
                xsrc, acc, tiles, lhs, sem_in, sem_out, *, n_d):
    b, e = pl.program_id(0), pl.program_id(1)
    n_b, n_e = pl.num_programs(0), pl.num_programs(1)
    C, D = lhs.shape
    half_f = wg_ref.shape[2] // 2

    def load(row):
        return pltpu.make_async_copy(x_hbm.at[row], xsrc, sem_in)

    def store(row):
        return pltpu.make_async_copy(acc, out_hbm.at[row], sem_out)

    @pl.when((b == 0) & (e == 0))
    def _():
        load(0).start()

    @pl.when(e == 0)
    def _():
        load(b).wait()

    def token_rows(t):
        return pl.ds(pl.multiple_of(t * n_d, n_d), n_d)

    def gather(g, carry):
        for i in range(GROUP):
            c = g * GROUP + i
            tiles[token_rows(c), :] = xsrc[token_rows(idx_ref[0, 0, c]), :]
        return carry

    lax.fori_loop(0, C // GROUP, gather, 0)

    @pl.when((e == n_e - 1) & (b + 1 < n_b))
    def _():
        load(b + 1).start()

    xs = [tiles[pl.ds(j, C, stride=n_d), :] for j in range(n_d)]
    rs = _inv_rms(xs, D)
    for j in range(n_d):
        cols = slice(j * LANES, (j + 1) * LANES)
        lhs[:, cols] = (tiles[pl.ds(j, C, stride=n_d), :] * rs * gf_ref[:, cols]).astype(BF16)

    h = lhs[...]
    y = None
    for f in range(2):
        fcols = slice(f * half_f, (f + 1) * half_f)
        a = jnp.dot(h, wg_ref[0, :, fcols], preferred_element_type=F32)
        u = jnp.dot(h, wu_ref[0, :, fcols], preferred_element_type=F32)
        hid = (a / (1.0 + jnp.exp(-a)) * u).astype(BF16)
        part = jnp.dot(hid, wd_ref[0, fcols, :], preferred_element_type=F32)
        y = part if y is None else y + part
    for j in range(n_d):
        tiles[pl.ds(j, C, stride=n_d), :] = y[:, j * LANES:(j + 1) * LANES]

    @pl.when(e == 0)
    def _():
        @pl.when(b > 0)
        def _():
            store(b - 1).wait()

        acc[...] = xsrc[...]

    def scatter(g, carry):
        updates = []
        for i in range(GROUP):
            c = g * GROUP + i
            rows = token_rows(idx_ref[0, 0, c])
            updates.append((rows, acc[rows, :] + tiles[token_rows(c), :] * gate_ref[0, 0, c]))
        for rows, val in updates:
            acc[rows, :] = val
        return carry

    lax.fori_loop(0, C // GROUP, scatter, 0)

    @pl.when(e == n_e - 1)
    def _():
        store(b).start()

    @pl.when((e == n_e - 1) & (b == n_b - 1))
    def _():
        store(b).wait()


def _moe(x_tt, idx, gate, gf, wg, wu, wd):
    B, rows, _ = x_tt.shape
    _, E, C = idx.shape
    D, Fd = wg.shape[1], wg.shape[2]
    n_d = D // LANES
    smem = lambda: pl.BlockSpec((1, 1, C), lambda b, e: (b * E + e, 0, 0), memory_space=pltpu.SMEM)
    weight = lambda shape: pl.BlockSpec((1,) + shape, lambda b, e: (e, 0, 0))
    return pl.pallas_call(
        functools.partial(_moe_kernel, n_d=n_d),
        grid=(B, E),
        in_specs=[smem(), smem(),
                  pl.BlockSpec(memory_space=pl.ANY),
                  pl.BlockSpec((1, D), lambda b, e: (0, 0)),
                  weight((D, Fd)), weight((D, Fd)), weight((Fd, D))],
        out_specs=pl.BlockSpec(memory_space=pl.ANY),
        out_shape=jax.ShapeDtypeStruct(x_tt.shape, F32),
        scratch_shapes=[pltpu.VMEM((rows, LANES), F32), pltpu.VMEM((rows, LANES), F32),
                        pltpu.VMEM((C * n_d, LANES), F32), pltpu.VMEM((C, D), BF16),
                        pltpu.SemaphoreType.DMA, pltpu.SemaphoreType.DMA],
        compiler_params=pltpu.CompilerParams(dimension_semantics=("arbitrary", "arbitrary"),
                                             vmem_limit_bytes=MOE_VMEM_LIMIT),
        name="moe",
    )(idx.reshape(B * E, 1, C), gate.reshape(B * E, 1, C), x_tt, gf.reshape(1, D), wg, wu, wd)


def _final_norm_kernel(x_ref, g_ref, o_ref):
    _, tm, D = o_ref.shape
    xs = _row_pieces(x_ref, tm, D // LANES, True)
    rs = _inv_rms(xs, D)
    for j, xj in enumerate(xs):
        cols = slice(j * LANES, (j + 1) * LANES)
        o_ref[0, :, cols] = xj * rs * g_ref[:, cols]


def _final_norm(x_tt, g, *, S, tm):
    B = x_tt.shape[0]
    D = g.shape[0]
    n_s = S // tm
    row = pl.BlockSpec((1, tm, D), lambda i: (i // n_s, i % n_s, 0))
    return pl.pallas_call(
        _final_norm_kernel,
        grid=(B * n_s,),
        in_specs=[pl.BlockSpec((1, tm * D // LANES, LANES), lambda i: (i // n_s, i % n_s, 0)),
                  pl.BlockSpec((1, D), lambda i: (0, 0))],
        out_specs=row,
        out_shape=jax.ShapeDtypeStruct((B, S, D), F32),
        compiler_params=pltpu.CompilerParams(dimension_semantics=("arbitrary",)),
        name="final_norm",
    )(x_tt, g.reshape(1, D))


def _rope_tables(S):
    inv = 1.0 / (ROPE_THETA ** (jnp.arange(0, HEAD_DIM, 2, dtype=F32) / HEAD_DIM))
    ang = jnp.arange(S, dtype=F32)[:, None] * inv[None, :]
    cos, sin = jnp.cos(ang), jnp.sin(ang)
    reps = LANES // HEAD_DIM
    cos_t = jnp.tile(jnp.concatenate([cos, cos], axis=-1), (1, reps))
    sin_t = jnp.tile(jnp.concatenate([-sin, sin], axis=-1), (1, reps))
    return cos_t, sin_t


def kernel(x, w_in, w_out, g_attn, g_mix_a, g_mix_b, sink, g_ffn, w_router, w_gate, w_up, w_down, g_final):
    B, S, D = x.shape
    depth = w_in.shape[0]
    tm = min(512, S)
    C = min(CAPACITY_FACTOR * S // N_EXPERTS, S)
    cos_t, sin_t = _rope_tables(S)
    for l in range(depth):
        tt = l > 0
        qa, ka, va, qb, kb, vb = _inproj(x, g_attn[l], w_in[l].astype(BF16), cos_t, sin_t,
                                         S=S, tm=tm, token_tiled=tt)
        oa = _attn_a(qa, ka, va)
        ob = _attn_b(qb, kb, vb, sink[l])
        x, aff = _outproj(oa, ob, x, g_mix_a[l], g_mix_b[l], w_out[l].astype(BF16),
                          g_ffn[l], w_router[l].T, S=S, tm=tm, token_tiled=tt)
        gate, idx = lax.top_k(aff, C)
        x = _moe(x, idx, gate, g_ffn[l], w_gate[l].astype(BF16), w_up[l].astype(BF16),
                 w_down[l].astype(BF16))
    return _final_norm(x, g_final, S=S, tm=tm)
```

```python
import functools

import jax
import jax.numpy as jnp
from jax import lax
from jax.experimental import pallas as pl
from jax.experimental.pallas import tpu as pltpu

HEAD_DIM = 64
LANES = 128
N_HEADS_A = 8
DILATIONS = ((128, 1), (512, 4), (2048, 16))
N_HEADS_B = 8
N_KV_B = 2
WINDOW_B = 128
WA = N_HEADS_A * HEAD_DIM
WB_Q = N_HEADS_B * HEAD_DIM
WB_KV = N_KV_B * HEAD_DIM
ROPE_THETA = 10000.0
N_EXPERTS = 16
CAPACITY_FACTOR = 2
RMS_EPS = 1e-6
NEG_INF = -1e30
LOG2E = 1.4426950408889634
VMEM_LIMIT = 56 * 1024 * 1024
MOE_VMEM_LIMIT = 60 * 1024 * 1024

F32 = jnp.float32
BF16 = jnp.bfloat16
_NT = (((1,), (1,)), ((), ()))


def _row_pieces(x_ref, tm, n, token_tiled, start=0):
    if token_tiled:
        return [x_ref[0, pl.ds(start * n + j, tm, stride=n), :] for j in range(n)]
    return [x_ref[0, start:start + tm, j * LANES:(j + 1) * LANES] for j in range(n)]


def _inv_rms(pieces, width):
    sq = pieces[0] * pieces[0]
    for p in pieces[1:]:
        sq = sq + p * p
    return lax.rsqrt(jnp.sum(sq, axis=-1, keepdims=True) / width + RMS_EPS)


def _lower_half_mask():
    return lax.broadcasted_iota(jnp.int32, (1, LANES), 1) < HEAD_DIM


def _inproj_kernel(x_ref, g_ref, w_ref, cos_ref, sin_ref,
                   qa_ref, ka_ref, va_ref, qb_ref, kb_ref, vb_ref, h_scr, *, token_tiled):
    tm, D = h_scr.shape
    xs = _row_pieces(x_ref, tm, D // LANES, token_tiled)
    rs = _inv_rms(xs, D)
    for j, xj in enumerate(xs):
        cols = slice(j * LANES, (j + 1) * LANES)
        h_scr[:, cols] = (xj * rs * g_ref[:, cols]).astype(BF16)
    cos = cos_ref[...]
    sin = sin_ref[...]
    lane = lax.broadcasted_iota(jnp.int32, (1, LANES), 1)
    first = (lane % HEAD_DIM) < (HEAD_DIM // 2)

    def rope(t, scale):
        partner = jnp.where(first, pltpu.roll(t, LANES - HEAD_DIM // 2, 1),
                            pltpu.roll(t, HEAD_DIM // 2, 1))
        return (t * cos + partner * sin) * scale

    q_scale = HEAD_DIM ** -0.5 * LOG2E
    n_a = WA // LANES
    n_b = WB_Q // LANES
    plan = (
        (qa_ref, True, 0 * n_a, n_a, True, q_scale),
        (ka_ref, True, 1 * n_a, n_a, True, 1.0),
        (va_ref, True, 2 * n_a, n_a, False, 1.0),
        (qb_ref, False, 3 * n_a, n_b, True, q_scale),
        (kb_ref, False, 3 * n_a + n_b, 1, True, 1.0),
        (vb_ref, False, 3 * n_a + n_b + 1, 1, False, 1.0),
    )
    h = h_scr[...]
    for ref, tile_major, start, count, do_rope, scale in plan:
        for c0 in range(0, count, 2):
            width = min(2, count - c0)
            col = (start + c0) * LANES
            t = jnp.dot(h, w_ref[:, col:col + width * LANES], preferred_element_type=F32)
            for j in range(width):
                tj = t[:, j * LANES:(j + 1) * LANES]
                val = (rope(tj, scale) if do_rope else tj).astype(BF16)
                if tile_major:
                    ref[0, c0 + j] = val
                else:
                    ref[0, :, (c0 + j) * LANES:(c0 + j + 1) * LANES] = val


def _inproj(x, g, w, cos, sin, *, S, tm, token_tiled):
    B = x.shape[0]
    D = w.shape[0]
    n_s = S // tm
    n_a = WA // LANES
    a_shape = jax.ShapeDtypeStruct((B, n_a, S, LANES), BF16)
    a_spec = pl.BlockSpec((1, n_a, tm, LANES), lambda i: (i // n_s, 0, i % n_s, 0))

    def row_spec(width):
        return pl.BlockSpec((1, tm, width), lambda i: (i // n_s, i % n_s, 0))

    tab_spec = pl.BlockSpec((tm, LANES), lambda i: (i % n_s, 0))
    x_spec = (pl.BlockSpec((1, tm * D // LANES, LANES), lambda i: (i // n_s, i % n_s, 0))
              if token_tiled else row_spec(D))
    return pl.pallas_call(
        functools.partial(_inproj_kernel, token_tiled=token_tiled),
        grid=(B * n_s,),
        in_specs=[x_spec,
                  pl.BlockSpec((1, D), lambda i: (0, 0)),
                  pl.BlockSpec(w.shape, lambda i: (0, 0)),
                  tab_spec, tab_spec],
        out_specs=[a_spec, a_spec, a_spec, row_spec(WB_Q), row_spec(WB_KV), row_spec(WB_KV)],
        out_shape=[a_shape, a_shape, a_shape,
                   jax.ShapeDtypeStruct((B, S, WB_Q), BF16),
                   jax.ShapeDtypeStruct((B, S, WB_KV), BF16),
                   jax.ShapeDtypeStruct((B, S, WB_KV), BF16)],
        scratch_shapes=[pltpu.VMEM((tm, D), BF16)],
        compiler_params=pltpu.CompilerParams(dimension_semantics=("arbitrary",),
                                             vmem_limit_bytes=VMEM_LIMIT),
        name="inproj",
    )(x, g.reshape(1, D), w, cos, sin)


def _band_bias(tq, kw, hw):
    row = jnp.arange(2 * tq)[None, :, None] % tq
    col = jnp.arange(kw)[None, None, :]
    off = (jnp.arange(3) * hw)[:, None, None]
    return jnp.where(jnp.abs(row - col + off) <= hw, 0.0, NEG_INF).astype(F32)


def _band_window(q0, hw, L, kw):
    ks = jnp.clip(q0 - hw, 0, L - kw)
    case = jnp.where(q0 - hw < 0, 0, jnp.where(q0 - hw > L - kw, 2, 1))
    return ks, case


def _band_block(q, k, v, bias, lo):
    tq = q.shape[0]
    zero = jnp.zeros_like(q)
    lhs = jnp.concatenate([jnp.where(lo, q, zero), jnp.where(lo, zero, q)], axis=0)
    s = lax.dot_general(lhs, k, _NT, preferred_element_type=F32) + bias
    m = jnp.max(s, axis=1, keepdims=True)
    e = jnp.exp2(s - m).astype(BF16)
    v_ones = jnp.concatenate([v, jnp.ones_like(v)], axis=1)
    pv = jnp.dot(e, v_ones, preferred_element_type=F32)
    acc2, l2 = pv[:, :LANES], pv[:, LANES:]
    acc = jnp.where(lo, acc2[:tq], acc2[tq:])
    m_t = jnp.where(lo, m[:tq], m[tq:])
    l_t = jnp.where(lo, l2[:tq], l2[tq:])
    return acc, m_t, l_t


def _pattern_geometry(S, tq):
    geo = []
    for window, d in DILATIONS:
        L = S // d
        hw = (window // 2) // d
        geo.append((d, L, hw, min(L, tq + 2 * hw), L // tq))
    return geo


def _attn_a_kernel(q_ref, k_ref, v_ref, bias_ref, o_ref, tmp, tmp2, qd, kd, vd, acc_s, m_s, l_s, *, S, tq):
    lo = _lower_half_mask()
    geo = _pattern_geometry(S, tq)
    for src, dst in ((q_ref, qd), (k_ref, kd), (v_ref, vd)):
        tmp[...] = src[0, 0].astype(F32)
        prev, nxt, d_prev, L_prev = tmp, tmp2, 1, S
        for p, (d, L, _, _, _) in enumerate(geo):
            if d == 1:
                continue
            keep_f32 = p + 1 < len(geo)

            def regroup(r, carry, dst=dst, p=p, L=L, q=d // d_prev, d_prev=d_prev, L_prev=L_prev,
                        prev=prev, nxt=nxt, keep_f32=keep_f32):
                rows = prev[pl.ds((r % d_prev) * L_prev + r // d_prev, L, stride=q), :]
                out = pl.ds(pl.multiple_of(r * L, L), L)
                dst[p - 1, out, :] = rows.astype(BF16)
                if keep_f32:
                    nxt[out, :] = rows
                return carry

            lax.fori_loop(0, d, regroup, 0)
            prev, nxt, d_prev, L_prev = nxt, prev, d, L

    def body(j, carry):
        for p, (d, L, hw, kw, n_qb) in enumerate(geo):
            r = j // n_qb
            q0 = (j % n_qb) * tq
            ks, case = _band_window(q0, hw, L, kw)
            if d == 1:
                rows_q = pl.ds(pl.multiple_of(q0, tq), tq)
                rows_k = pl.ds(pl.multiple_of(ks, hw), kw)
                q, k, v = q_ref[0, 0, rows_q, :], k_ref[0, 0, rows_k, :], v_ref[0, 0, rows_k, :]
            else:
                rows_q = pl.ds(r + d * q0, tq, stride=d)
                dense_q = pl.ds(pl.multiple_of(r * L + q0, tq), tq)
                dense_k = pl.ds(pl.multiple_of(r * L + ks, hw), kw)
                q, k, v = qd[p - 1, dense_q, :], kd[p - 1, dense_k, :], vd[p - 1, dense_k, :]
            acc, m_t, l_t = _band_block(q, k, v, bias_ref[case], lo)
            acc_s[p, rows_q, :] = acc
            m_s[p, rows_q, :] = m_t
            l_s[p, rows_q, :] = l_t
        return carry

    lax.fori_loop(0, S // tq, body, 0, unroll=4)

    def merge(j, carry):
        rows = pl.ds(pl.multiple_of(j * tq, tq), tq)
        ms = [m_s[p, rows, :] for p in range(len(geo))]
        m = functools.reduce(jnp.maximum, ms)
        num = den = None
        for p, mp in enumerate(ms):
            w = jnp.exp2(mp - m)
            num = acc_s[p, rows, :] * w if num is None else num + acc_s[p, rows, :] * w
            den = l_s[p, rows, :] * w if den is None else den + l_s[p, rows, :] * w
        o_ref[0, 0, rows, :] = (num / den).astype(BF16)
        return carry

    lax.fori_loop(0, S // tq, merge, 0)


def _attn_a(qa, ka, va, *, tq=128):
    B, n_a, S, _ = qa.shape
    geo = _pattern_geometry(S, tq)
    _, _, hw, kw, _ = geo[0]
    assert all(g[2] == hw and g[3] == kw and g[1] % tq == 0 for g in geo)
    bias = _band_bias(tq, kw, hw)
    spec = pl.BlockSpec((1, 1, S, LANES), lambda b, c: (b, c, 0, 0))
    assert geo[0][0] == 1
    seq = pltpu.VMEM((S, LANES), F32)
    dilated = pltpu.VMEM((len(geo) - 1, S, LANES), BF16)
    state = pltpu.VMEM((len(geo), S, LANES), F32)
    return pl.pallas_call(
        functools.partial(_attn_a_kernel, S=S, tq=tq),
        grid=(B, n_a),
        in_specs=[spec, spec, spec, pl.BlockSpec(bias.shape, lambda b, c: (0, 0, 0))],
        out_specs=spec,
        out_shape=jax.ShapeDtypeStruct((B, n_a, S, LANES), BF16),
        scratch_shapes=[seq, seq, dilated, dilated, dilated, state, state, state],
        compiler_params=pltpu.CompilerParams(dimension_semantics=("arbitrary", "arbitrary"),
                                             vmem_limit_bytes=VMEM_LIMIT),
        name="attn_a",
    )(qa, ka, va, bias)


def _attn_b_kernel(q_ref, k_ref, v_ref, sink_ref, bias_ref, o_ref, kk, vv, *, S, tq):
    lo = _lower_half_mask()
    hw = WINDOW_B
    kw = min(S, tq + 2 * hw)
    for src, dst in ((k_ref, kk), (v_ref, vv)):
        t = src[0].astype(F32)
        sw = pltpu.roll(t, HEAD_DIM, 1)
        dst[0] = jnp.where(lo, t, sw).astype(BF16)
        dst[1] = jnp.where(lo, sw, t).astype(BF16)
    n_c = WB_Q // LANES
    tiles_per_kv = n_c // N_KV_B

    def body(i, carry):
        q0 = pl.multiple_of(i * tq, tq)
        ks, case = _band_window(q0, hw, S, kw)
        rows_q = pl.ds(q0, tq)
        rows_k = pl.ds(pl.multiple_of(ks, tq), kw)
        bias = bias_ref[case]
        for c in range(n_c):
            cols = slice(c * LANES, (c + 1) * LANES)
            j = c // tiles_per_kv
            acc, m_t, l_t = _band_block(q_ref[0, rows_q, cols], kk[j, rows_k, :], vv[j, rows_k, :],
                                        bias, lo)
            sk = sink_ref[c:c + 1, :]
            m2 = jnp.maximum(m_t, sk)
            a = jnp.exp2(m_t - m2)
            den = l_t * a + jnp.exp2(sk - m2)
            o_ref[0, rows_q, cols] = (acc * a / den).astype(BF16)
        return carry

    lax.fori_loop(0, S // tq, body, 0, unroll=2)


def _attn_b(qb, kb, vb, sink, *, tq=128):
    B, S, _ = qb.shape
    n_c = WB_Q // LANES
    kw = min(S, tq + 2 * WINDOW_B)
    assert S % tq == 0 and tq == WINDOW_B
    bias = _band_bias(tq, kw, WINDOW_B)
    sink_tab = jnp.repeat(sink.reshape(n_c, LANES // HEAD_DIM).astype(F32) * LOG2E, HEAD_DIM, axis=1)
    kv_spec = pl.BlockSpec((1, S, WB_KV), lambda b: (b, 0, 0))
    q_spec = pl.BlockSpec((1, S, WB_Q), lambda b: (b, 0, 0))
    return pl.pallas_call(
        functools.partial(_attn_b_kernel, S=S, tq=tq),
        grid=(B,),
        in_specs=[q_spec, kv_spec, kv_spec, pl.BlockSpec((n_c, LANES), lambda b: (0, 0)),
                  pl.BlockSpec(bias.shape, lambda b: (0, 0, 0))],
        out_specs=q_spec,
        out_shape=jax.ShapeDtypeStruct((B, S, WB_Q), BF16),
        scratch_shapes=[pltpu.VMEM((N_KV_B, S, LANES), BF16)] * 2,
        compiler_params=pltpu.CompilerParams(dimension_semantics=("arbitrary",),
                                             vmem_limit_bytes=VMEM_LIMIT),
        name="attn_b",
    )(qb, kb, vb, sink_tab, bias)


def _outproj_kernel(oa_ref, ob_ref, x_ref, ga_ref, gb_ref, w_ref, gf_ref, wr_ref,
                    x_out_ref, aff_ref, mix_scr, *, token_tiled):
    tm, D = mix_scr.shape[0], w_ref.shape[1]
    n_d = D // LANES
    n_a = WA // LANES
    E = wr_ref.shape[0] // 2
    a = [oa_ref[0, c].astype(F32) for c in range(n_a)]
    rs_a = _inv_rms(a, WA)
    for c in range(n_a):
        cols = slice(c * LANES, (c + 1) * LANES)
        mix_scr[:, cols] = (a[c] * rs_a * ga_ref[:, cols]).astype(BF16)
    b = ob_ref[0].astype(F32)
    rs_b = lax.rsqrt(jnp.mean(b * b, axis=-1, keepdims=True) + RMS_EPS)
    mix_scr[:, WA:] = (b * rs_b * gb_ref[...]).astype(BF16)
    y = jnp.dot(mix_scr[...], w_ref[...], preferred_element_type=F32)
    xs = [xj + y[:, j * LANES:(j + 1) * LANES]
          for j, xj in enumerate(_row_pieces(x_ref, tm, n_d, token_tiled))]
    for j, xj in enumerate(xs):
        x_out_ref[0, pl.ds(j, tm, stride=n_d), :] = xj
    rs = _inv_rms(xs, D)
    h2 = jnp.concatenate([xj * rs * gf_ref[:, j * LANES:(j + 1) * LANES]
                          for j, xj in enumerate(xs)], axis=1)
    h_hi = h2.astype(BF16)
    h_lo = (h2 - h_hi.astype(F32)).astype(BF16)
    p = lax.dot_general(wr_ref[...], h_hi, _NT, preferred_element_type=F32)
    logits = p[:E] + p[E:] + lax.dot_general(wr_ref[:E, :], h_lo, _NT, preferred_element_type=F32)
    z = jnp.exp(logits - jnp.max(logits, axis=0, keepdims=True))
    aff_ref[0] = z / jnp.sum(z, axis=0, keepdims=True)


def _outproj(oa, ob, x, ga, gb, w, gf, w_router, *, S, tm, token_tiled):
    B = x.shape[0]
    D = w.shape[1]
    n_s = S // tm
    n_a = WA // LANES
    E = w_router.shape[1]
    wr_hi = w_router.T.astype(BF16)
    wr_lo = (w_router.T - wr_hi.astype(F32)).astype(BF16)
    wr_t = jnp.concatenate([wr_hi, wr_lo], axis=0)
    const = lambda shape: pl.BlockSpec(shape, lambda i: (0,) * len(shape))
    row = lambda width: pl.BlockSpec((1, tm, width), lambda i: (i // n_s, i % n_s, 0))
    tiled = pl.BlockSpec((1, tm * D // LANES, LANES), lambda i: (i // n_s, i % n_s, 0))
    return pl.pallas_call(
        functools.partial(_outproj_kernel, token_tiled=token_tiled),
        grid=(B * n_s,),
        in_specs=[pl.BlockSpec((1, n_a, tm, LANES), lambda i: (i // n_s, 0, i % n_s, 0)),
                  row(WB_Q), tiled if token_tiled else row(D), const((1, WA)), const((1, WB_Q)), const(w.shape),
                  const((1, D)), const(wr_t.shape)],
        out_specs=[tiled, pl.BlockSpec((1, E, tm), lambda i: (i // n_s, 0, i % n_s))],
        out_shape=[jax.ShapeDtypeStruct((B, S * D // LANES, LANES), F32),
                   jax.ShapeDtypeStruct((B, E, S), F32)],
        scratch_shapes=[pltpu.VMEM((tm, WA + WB_Q), BF16)],
        compiler_params=pltpu.CompilerParams(dimension_semantics=("arbitrary",),
                                             vmem_limit_bytes=VMEM_LIMIT),
        name="outproj",
    )(oa, ob, x, ga.reshape(1, WA), gb.reshape(1, WB_Q), w, gf.reshape(1, D), wr_t)


def _select_kernel(aff_ref, tri_ref, idx_ref, cnt_scr, *, C):
    aff = aff_ref[0]
    E, S = aff.shape
    n_t = S // LANES
    bits = pltpu.bitcast(aff, jnp.int32)

    def search(i, t):
        cand = t | jnp.left_shift(jnp.int32(1), 30 - i)
        n_ge = jnp.sum((bits >= cand).astype(F32), axis=1, keepdims=True)
        return jnp.where(n_ge >= C, cand, t)

    t = lax.fori_loop(0, 31, search, jnp.zeros((E, 1), jnp.int32))
    above = bits > t
    tied = bits == t
    n_ties = C - jnp.sum(above.astype(F32), axis=1, keepdims=True)

    def running_count(mask):
        stacked = jnp.concatenate([mask[:, k * LANES:(k + 1) * LANES] for k in range(n_t)], axis=0)
        inside = jnp.dot(stacked.astype(BF16), tri_ref[...], preferred_element_type=F32)
        out, carry = [], jnp.zeros((E, 1), F32)
        for k in range(n_t):
            tile = inside[k * E:(k + 1) * E] + carry
            out.append(tile)
            carry = tile[:, LANES - 1:]
        return jnp.concatenate(out, axis=1)

    chosen = above | (tied & (running_count(tied) <= n_ties))
    cnt_scr[...] = running_count(chosen)
    c_col = lax.broadcasted_iota(jnp.int32, (C, 1), 0).astype(F32)
    lane = lax.broadcasted_iota(jnp.int32, (1, LANES), 1)

    def invert(e, acc):
        n_before = jnp.sum((cnt_scr[pl.ds(e, 1), :] <= c_col).astype(F32), axis=1, keepdims=True)
        return jnp.where(lane == e, n_before, acc)

    table = lax.fori_loop(0, E, invert, jnp.zeros((C, LANES), F32))
    idx_ref[0] = table.T[:E].astype(jnp.int32)


def _select(aff, C):
    B, E, S = aff.shape
    tri = (jnp.arange(LANES)[:, None] <= jnp.arange(LANES)[None, :]).astype(BF16)
    return pl.pallas_call(
        functools.partial(_select_kernel, C=C),
        grid=(B,),
        in_specs=[pl.BlockSpec((1, E, S), lambda b: (b, 0, 0)),
                  pl.BlockSpec((LANES, LANES), lambda b: (0, 0))],
        out_specs=pl.BlockSpec((1, E, C), lambda b: (b, 0, 0)),
        out_shape=jax.ShapeDtypeStruct((B, E, C), jnp.int32),
        scratch_shapes=[pltpu.VMEM((E, S), F32)],
        compiler_params=pltpu.CompilerParams(dimension_semantics=("arbitrary",),
                                             vmem_limit_bytes=VMEM_LIMIT),
        name="select",
    )(aff, tri)


GROUP = 8


def _moe_kernel(idx_ref, aff_ref, x_hbm, gf_ref, wg_ref, wu_ref, wd_ref, out_hbm,
                xsrc, acc, tiles, lhs, sem_in, sem_out, *, n_d):
    b, e = pl.program_id(0), pl.program_id(1)
    n_b, n_e = pl.num_programs(0), pl.num_programs(1)
    C, D = lhs.shape
    half_f = wg_ref.shape[2] // 2

    def load(row):
        return pltpu.make_async_copy(x_hbm.at[row], xsrc, sem_in)

    def store(row):
        return pltpu.make_async_copy(acc, out_hbm.at[row], sem_out)

    @pl.when((b == 0) & (e == 0))
    def _():
        load(0).start()

    @pl.when(e == 0)
    def _():
        load(b).wait()

    def token_rows(t):
        return pl.ds(pl.multiple_of(t * n_d, n_d), n_d)

    def gather(g, carry):
        for i in range(GROUP):
            c = g * GROUP + i
            tiles[token_rows(c), :] = xsrc[token_rows(idx_ref[0, 0, c]), :]
        return carry

    lax.fori_loop(0, C // GROUP, gather, 0)

    @pl.when((e == n_e - 1) & (b + 1 < n_b))
    def _():
        load(b + 1).start()

    xs = [tiles[pl.ds(j, C, stride=n_d), :] for j in range(n_d)]
    rs = _inv_rms(xs, D)
    for j in range(n_d):
        cols = slice(j * LANES, (j + 1) * LANES)
        lhs[:, cols] = (tiles[pl.ds(j, C, stride=n_d), :] * rs * gf_ref[:, cols]).astype(BF16)

    h = lhs[...]
    y = None
    for f in range(2):
        fcols = slice(f * half_f, (f + 1) * half_f)
        a = jnp.dot(h, wg_ref[0, :, fcols], preferred_element_type=F32)
        u = jnp.dot(h, wu_ref[0, :, fcols], preferred_element_type=F32)
        hid = (a / (1.0 + jnp.exp(-a)) * u).astype(BF16)
        part = jnp.dot(hid, wd_ref[0, fcols, :], preferred_element_type=F32)
        y = part if y is None else y + part
    for j in range(n_d):
        tiles[pl.ds(j, C, stride=n_d), :] = y[:, j * LANES:(j + 1) * LANES]

    @pl.when(e == 0)
    def _():
        @pl.when(b > 0)
        def _():
            store(b - 1).wait()

        acc[...] = xsrc[...]

    def scatter(g, carry):
        updates = []
        for i in range(GROUP):
            c = g * GROUP + i
            t = idx_ref[0, 0, c]
            rows = token_rows(t)
            updates.append((rows, acc[rows, :] + tiles[token_rows(c), :] * aff_ref[0, 0, t]))
        for rows, val in updates:
            acc[rows, :] = val
        return carry

    lax.fori_loop(0, C // GROUP, scatter, 0)

    @pl.when(e == n_e - 1)
    def _():
        store(b).start()

    @pl.when((e == n_e - 1) & (b == n_b - 1))
    def _():
        store(b).wait()


def _moe(x_tt, idx, aff, gf, wg, wu, wd):
    B, rows, _ = x_tt.shape
    _, E, C = idx.shape
    S = aff.shape[2]
    D, Fd = wg.shape[1], wg.shape[2]
    n_d = D // LANES
    smem = lambda n: pl.BlockSpec((1, 1, n), lambda b, e: (b * E + e, 0, 0), memory_space=pltpu.SMEM)
    weight = lambda shape: pl.BlockSpec((1,) + shape, lambda b, e: (e, 0, 0))
    return pl.pallas_call(
        functools.partial(_moe_kernel, n_d=n_d),
        grid=(B, E),
        in_specs=[smem(C), smem(S),
                  pl.BlockSpec(memory_space=pl.ANY),
                  pl.BlockSpec((1, D), lambda b, e: (0, 0)),
                  weight((D, Fd)), weight((D, Fd)), weight((Fd, D))],
        out_specs=pl.BlockSpec(memory_space=pl.ANY),
        out_shape=jax.ShapeDtypeStruct(x_tt.shape, F32),
        scratch_shapes=[pltpu.VMEM((rows, LANES), F32), pltpu.VMEM((rows, LANES), F32),
                        pltpu.VMEM((C * n_d, LANES), F32), pltpu.VMEM((C, D), BF16),
                        pltpu.SemaphoreType.DMA, pltpu.SemaphoreType.DMA],
        compiler_params=pltpu.CompilerParams(dimension_semantics=("arbitrary", "arbitrary"),
                                             vmem_limit_bytes=MOE_VMEM_LIMIT),
        name="moe",
    )(idx.reshape(B * E, 1, C), aff.reshape(B * E, 1, S), x_tt, gf.reshape(1, D), wg, wu, wd)


def _final_norm_kernel(x_ref, g_ref, o_ref):
    _, tm, D = o_ref.shape
    xs = _row_pieces(x_ref, tm, D // LANES, True)
    rs = _inv_rms(xs, D)
    for j, xj in enumerate(xs):
        cols = slice(j * LANES, (j + 1) * LANES)
        o_ref[0, :, cols] = xj * rs * g_ref[:, cols]


def _final_norm(x_tt, g, *, S, tm):
    B = x_tt.shape[0]
    D = g.shape[0]
    n_s = S // tm
    row = pl.BlockSpec((1, tm, D), lambda i: (i // n_s, i % n_s, 0))
    return pl.pallas_call(
        _final_norm_kernel,
        grid=(B * n_s,),
        in_specs=[pl.BlockSpec((1, tm * D // LANES, LANES), lambda i: (i // n_s, i % n_s, 0)),
                  pl.BlockSpec((1, D), lambda i: (0, 0))],
        out_specs=row,
        out_shape=jax.ShapeDtypeStruct((B, S, D), F32),
        compiler_params=pltpu.CompilerParams(dimension_semantics=("arbitrary",)),
        name="final_norm",
    )(x_tt, g.reshape(1, D))


def _rope_tables(S):
    inv = 1.0 / (ROPE_THETA ** (jnp.arange(0, HEAD_DIM, 2, dtype=F32) / HEAD_DIM))
    ang = jnp.arange(S, dtype=F32)[:, None] * inv[None, :]
    cos, sin = jnp.cos(ang), jnp.sin(ang)
    reps = LANES // HEAD_DIM
    cos_t = jnp.tile(jnp.concatenate([cos, cos], axis=-1), (1, reps))
    sin_t = jnp.tile(jnp.concatenate([-sin, sin], axis=-1), (1, reps))
    return cos_t, sin_t


def kernel(x, w_in, w_out, g_attn, g_mix_a, g_mix_b, sink, g_ffn, w_router, w_gate, w_up, w_down, g_final):
    B, S, D = x.shape
    depth = w_in.shape[0]
    tm = min(512, S)
    C = min(CAPACITY_FACTOR * S // N_EXPERTS, S)
    cos_t, sin_t = _rope_tables(S)
    for l in range(depth):
        tt = l > 0
        qa, ka, va, qb, kb, vb = _inproj(x, g_attn[l], w_in[l].astype(BF16), cos_t, sin_t,
                                         S=S, tm=tm, token_tiled=tt)
        oa = _attn_a(qa, ka, va)
        ob = _attn_b(qb, kb, vb, sink[l])
        x, aff = _outproj(oa, ob, x, g_mix_a[l], g_mix_b[l], w_out[l].astype(BF16),
                          g_ffn[l], w_router[l], S=S, tm=tm, token_tiled=tt)
        x = _moe(x, _select(aff, C), aff, g_ffn[l], w_gate[l].astype(BF16), w_up[l].astype(BF16),
                 w_down[l].astype(BF16))
    return _final_norm(x, g_final, S=S, tm=tm)
```

```python
import functools

import jax
import jax.numpy as jnp
from jax import lax
from jax.experimental import pallas as pl
from jax.experimental.pallas import tpu as pltpu

HEAD_DIM = 64
LANES = 128
N_HEADS_A = 8
DILATIONS = ((128, 1), (512, 4), (2048, 16))
N_HEADS_B = 8
N_KV_B = 2
WINDOW_B = 128
WA = N_HEADS_A * HEAD_DIM
WB_Q = N_HEADS_B * HEAD_DIM
WB_KV = N_KV_B * HEAD_DIM
ROPE_THETA = 10000.0
N_EXPERTS = 16
CAPACITY_FACTOR = 2
RMS_EPS = 1e-6
NEG_INF = -1e30
LOG2E = 1.4426950408889634
VMEM_LIMIT = 56 * 1024 * 1024
MOE_VMEM_LIMIT = 60 * 1024 * 1024

F32 = jnp.float32
BF16 = jnp.bfloat16
_NT = (((1,), (1,)), ((), ()))


def _row_pieces(x_ref, tm, n, token_tiled, start=0):
    if token_tiled:
        return [x_ref[0, pl.ds(start * n + j, tm, stride=n), :] for j in range(n)]
    return [x_ref[0, start:start + tm, j * LANES:(j + 1) * LANES] for j in range(n)]


def _inv_rms(pieces, width):
    sq = pieces[0] * pieces[0]
    for p in pieces[1:]:
        sq = sq + p * p
    return lax.rsqrt(jnp.sum(sq, axis=-1, keepdims=True) / width + RMS_EPS)


def _lower_half_mask():
    return lax.broadcasted_iota(jnp.int32, (1, LANES), 1) < HEAD_DIM


def _inproj_kernel(x_ref, g_ref, w_ref, cos_ref, sin_ref,
                   qa_ref, ka_ref, va_ref, qb_ref, kb_ref, vb_ref, h_scr, *, token_tiled):
    tm, D = h_scr.shape
    xs = _row_pieces(x_ref, tm, D // LANES, token_tiled)
    rs = _inv_rms(xs, D)
    for j, xj in enumerate(xs):
        cols = slice(j * LANES, (j + 1) * LANES)
        h_scr[:, cols] = (xj * rs * g_ref[:, cols]).astype(BF16)
    cos = cos_ref[...]
    sin = sin_ref[...]
    lane = lax.broadcasted_iota(jnp.int32, (1, LANES), 1)
    first = (lane % HEAD_DIM) < (HEAD_DIM // 2)

    def rope(t, scale):
        partner = jnp.where(first, pltpu.roll(t, LANES - HEAD_DIM // 2, 1),
                            pltpu.roll(t, HEAD_DIM // 2, 1))
        return (t * cos + partner * sin) * scale

    q_scale = HEAD_DIM ** -0.5 * LOG2E
    n_a = WA // LANES
    n_b = WB_Q // LANES
    plan = (
        (qa_ref, True, 0 * n_a, n_a, True, q_scale),
        (ka_ref, True, 1 * n_a, n_a, True, 1.0),
        (va_ref, True, 2 * n_a, n_a, False, 1.0),
        (qb_ref, False, 3 * n_a, n_b, True, q_scale),
        (kb_ref, False, 3 * n_a + n_b, 1, True, 1.0),
        (vb_ref, False, 3 * n_a + n_b + 1, 1, False, 1.0),
    )
    h = h_scr[...]
    for ref, tile_major, start, count, do_rope, scale in plan:
        for c0 in range(0, count, 2):
            width = min(2, count - c0)
            col = (start + c0) * LANES
            t = jnp.dot(h, w_ref[:, col:col + width * LANES], preferred_element_type=F32)
            for j in range(width):
                tj = t[:, j * LANES:(j + 1) * LANES]
                val = (rope(tj, scale) if do_rope else tj).astype(BF16)
                if tile_major:
                    ref[0, c0 + j] = val
                else:
                    ref[0, :, (c0 + j) * LANES:(c0 + j + 1) * LANES] = val


def _inproj(x, g, w, cos, sin, *, S, tm, token_tiled):
    B = x.shape[0]
    D = w.shape[0]
    n_s = S // tm
    n_a = WA // LANES
    a_shape = jax.ShapeDtypeStruct((B, n_a, S, LANES), BF16)
    a_spec = pl.BlockSpec((1, n_a, tm, LANES), lambda i: (i // n_s, 0, i % n_s, 0))

    def row_spec(width):
        return pl.BlockSpec((1, tm, width), lambda i: (i // n_s, i % n_s, 0))

    tab_spec = pl.BlockSpec((tm, LANES), lambda i: (i % n_s, 0))
    x_spec = (pl.BlockSpec((1, tm * D // LANES, LANES), lambda i: (i // n_s, i % n_s, 0))
              if token_tiled else row_spec(D))
    return pl.pallas_call(
        functools.partial(_inproj_kernel, token_tiled=token_tiled),
        grid=(B * n_s,),
        in_specs=[x_spec,
                  pl.BlockSpec((1, D), lambda i: (0, 0)),
                  pl.BlockSpec(w.shape, lambda i: (0, 0)),
                  tab_spec, tab_spec],
        out_specs=[a_spec, a_spec, a_spec, row_spec(WB_Q), row_spec(WB_KV), row_spec(WB_KV)],
        out_shape=[a_shape, a_shape, a_shape,
                   jax.ShapeDtypeStruct((B, S, WB_Q), BF16),
                   jax.ShapeDtypeStruct((B, S, WB_KV), BF16),
                   jax.ShapeDtypeStruct((B, S, WB_KV), BF16)],
        scratch_shapes=[pltpu.VMEM((tm, D), BF16)],
        compiler_params=pltpu.CompilerParams(dimension_semantics=("arbitrary",),
                                             vmem_limit_bytes=VMEM_LIMIT),
        name="inproj",
    )(x, g.reshape(1, D), w, cos, sin)


def _band_bias(tq, kw, hw):
    row = jnp.arange(2 * tq)[None, :, None] % tq
    col = jnp.arange(kw)[None, None, :]
    off = (jnp.arange(3) * hw)[:, None, None]
    return jnp.where(jnp.abs(row - col + off) <= hw, 0.0, NEG_INF).astype(F32)


def _band_window(q0, hw, L, kw):
    ks = jnp.clip(q0 - hw, 0, L - kw)
    case = jnp.where(q0 - hw < 0, 0, jnp.where(q0 - hw > L - kw, 2, 1))
    return ks, case


def _band_block(q, k, v, bias, lo):
    tq = q.shape[0]
    zero = jnp.zeros_like(q)
    lhs = jnp.concatenate([jnp.where(lo, q, zero), jnp.where(lo, zero, q)], axis=0)
    s = lax.dot_general(lhs, k, _NT, preferred_element_type=F32) + bias
    m = jnp.max(s, axis=1, keepdims=True)
    e = jnp.exp2(s - m).astype(BF16)
    v_ones = jnp.concatenate([v, jnp.ones_like(v)], axis=1)
    pv = jnp.dot(e, v_ones, preferred_element_type=F32)
    acc2, l2 = pv[:, :LANES], pv[:, LANES:]
    acc = jnp.where(lo, acc2[:tq], acc2[tq:])
    m_t = jnp.where(lo, m[:tq], m[tq:])
    l_t = jnp.where(lo, l2[:tq], l2[tq:])
    return acc, m_t, l_t


def _pattern_geometry(S, tq):
    geo = []
    for window, d in DILATIONS:
        L = S // d
        hw = (window // 2) // d
        geo.append((d, L, hw, min(L, tq + 2 * hw), L // tq))
    return geo


def _attn_a_kernel(q_ref, k_ref, v_ref, bias_ref, o_ref, tmp, tmp2, qd, kd, vd, acc_s, m_s, l_s, *, S, tq):
    lo = _lower_half_mask()
    geo = _pattern_geometry(S, tq)
    for src, dst in ((q_ref, qd), (k_ref, kd), (v_ref, vd)):
        tmp[...] = src[0, 0].astype(F32)
        prev, nxt, d_prev, L_prev = tmp, tmp2, 1, S
        for p, (d, L, _, _, _) in enumerate(geo):
            if d == 1:
                continue
            keep_f32 = p + 1 < len(geo)

            def regroup(r, carry, dst=dst, p=p, L=L, q=d // d_prev, d_prev=d_prev, L_prev=L_prev,
                        prev=prev, nxt=nxt, keep_f32=keep_f32):
                rows = prev[pl.ds((r % d_prev) * L_prev + r // d_prev, L, stride=q), :]
                out = pl.ds(pl.multiple_of(r * L, L), L)
                dst[p - 1, out, :] = rows.astype(BF16)
                if keep_f32:
                    nxt[out, :] = rows
                return carry

            lax.fori_loop(0, d, regroup, 0)
            prev, nxt, d_prev, L_prev = nxt, prev, d, L

    def body(j, carry):
        for p, (d, L, hw, kw, n_qb) in enumerate(geo):
            r = j // n_qb
            q0 = (j % n_qb) * tq
            ks, case = _band_window(q0, hw, L, kw)
            if d == 1:
                rows_q = pl.ds(pl.multiple_of(q0, tq), tq)
                rows_k = pl.ds(pl.multiple_of(ks, hw), kw)
                q, k, v = q_ref[0, 0, rows_q, :], k_ref[0, 0, rows_k, :], v_ref[0, 0, rows_k, :]
            else:
                rows_q = pl.ds(r + d * q0, tq, stride=d)
                dense_q = pl.ds(pl.multiple_of(r * L + q0, tq), tq)
                dense_k = pl.ds(pl.multiple_of(r * L + ks, hw), kw)
                q, k, v = qd[p - 1, dense_q, :], kd[p - 1, dense_k, :], vd[p - 1, dense_k, :]
            acc, m_t, l_t = _band_block(q, k, v, bias_ref[case], lo)
            acc_s[p, rows_q, :] = acc
            m_s[p, rows_q, :] = m_t
            l_s[p, rows_q, :] = l_t
        return carry

    lax.fori_loop(0, S // tq, body, 0, unroll=8)

    def merge(j, carry):
        rows = pl.ds(pl.multiple_of(j * tq, tq), tq)
        ms = [m_s[p, rows, :] for p in range(len(geo))]
        m = functools.reduce(jnp.maximum, ms)
        num = den = None
        for p, mp in enumerate(ms):
            w = jnp.exp2(mp - m)
            num = acc_s[p, rows, :] * w if num is None else num + acc_s[p, rows, :] * w
            den = l_s[p, rows, :] * w if den is None else den + l_s[p, rows, :] * w
        o_ref[0, 0, rows, :] = (num / den).astype(BF16)
        return carry

    lax.fori_loop(0, S // tq, merge, 0, unroll=2)


def _attn_a(qa, ka, va, *, tq=128):
    B, n_a, S, _ = qa.shape
    geo = _pattern_geometry(S, tq)
    _, _, hw, kw, _ = geo[0]
    assert all(g[2] == hw and g[3] == kw and g[1] % tq == 0 for g in geo)
    bias = _band_bias(tq, kw, hw)
    spec = pl.BlockSpec((1, 1, S, LANES), lambda b, c: (b, c, 0, 0))
    assert geo[0][0] == 1
    seq = pltpu.VMEM((S, LANES), F32)
    dilated = pltpu.VMEM((len(geo) - 1, S, LANES), BF16)
    state = pltpu.VMEM((len(geo), S, LANES), F32)
    return pl.pallas_call(
        functools.partial(_attn_a_kernel, S=S, tq=tq),
        grid=(B, n_a),
        in_specs=[spec, spec, spec, pl.BlockSpec(bias.shape, lambda b, c: (0, 0, 0))],
        out_specs=spec,
        out_shape=jax.ShapeDtypeStruct((B, n_a, S, LANES), BF16),
        scratch_shapes=[seq, seq, dilated, dilated, dilated, state, state, state],
        compiler_params=pltpu.CompilerParams(dimension_semantics=("arbitrary", "arbitrary"),
                                             vmem_limit_bytes=VMEM_LIMIT),
        name="attn_a",
    )(qa, ka, va, bias)


def _attn_b_kernel(q_ref, k_ref, v_ref, sink_ref, bias_ref, o_ref, kk, vv, *, S, tq):
    lo = _lower_half_mask()
    hw = WINDOW_B
    kw = min(S, tq + 2 * hw)
    for src, dst in ((k_ref, kk), (v_ref, vv)):
        t = src[0].astype(F32)
        sw = pltpu.roll(t, HEAD_DIM, 1)
        dst[0] = jnp.where(lo, t, sw).astype(BF16)
        dst[1] = jnp.where(lo, sw, t).astype(BF16)
    n_c = WB_Q // LANES
    tiles_per_kv = n_c // N_KV_B

    def body(i, carry):
        q0 = pl.multiple_of(i * tq, tq)
        ks, case = _band_window(q0, hw, S, kw)
        rows_q = pl.ds(q0, tq)
        rows_k = pl.ds(pl.multiple_of(ks, tq), kw)
        bias = bias_ref[case]
        for c in range(n_c):
            cols = slice(c * LANES, (c + 1) * LANES)
            j = c // tiles_per_kv
            acc, m_t, l_t = _band_block(q_ref[0, rows_q, cols], kk[j, rows_k, :], vv[j, rows_k, :],
                                        bias, lo)
            sk = sink_ref[c:c + 1, :]
            m2 = jnp.maximum(m_t, sk)
            a = jnp.exp2(m_t - m2)
            den = l_t * a + jnp.exp2(sk - m2)
            o_ref[0, rows_q, cols] = (acc * a / den).astype(BF16)
        return carry

    lax.fori_loop(0, S // tq, body, 0, unroll=4)


def _attn_b(qb, kb, vb, sink, *, tq=128):
    B, S, _ = qb.shape
    n_c = WB_Q // LANES
    kw = min(S, tq + 2 * WINDOW_B)
    assert S % tq == 0 and tq == WINDOW_B
    bias = _band_bias(tq, kw, WINDOW_B)
    sink_tab = jnp.repeat(sink.reshape(n_c, LANES // HEAD_DIM).astype(F32) * LOG2E, HEAD_DIM, axis=1)
    kv_spec = pl.BlockSpec((1, S, WB_KV), lambda b: (b, 0, 0))
    q_spec = pl.BlockSpec((1, S, WB_Q), lambda b: (b, 0, 0))
    return pl.pallas_call(
        functools.partial(_attn_b_kernel, S=S, tq=tq),
        grid=(B,),
        in_specs=[q_spec, kv_spec, kv_spec, pl.BlockSpec((n_c, LANES), lambda b: (0, 0)),
                  pl.BlockSpec(bias.shape, lambda b: (0, 0, 0))],
        out_specs=q_spec,
        out_shape=jax.ShapeDtypeStruct((B, S, WB_Q), BF16),
        scratch_shapes=[pltpu.VMEM((N_KV_B, S, LANES), BF16)] * 2,
        compiler_params=pltpu.CompilerParams(dimension_semantics=("arbitrary",),
                                             vmem_limit_bytes=VMEM_LIMIT),
        name="attn_b",
    )(qb, kb, vb, sink_tab, bias)


def _outproj_kernel(oa_ref, ob_ref, x_ref, ga_ref, gb_ref, w_ref, gf_ref, wr_ref,
                    x_out_ref, aff_ref, mix_scr, *, token_tiled):
    tm, D = mix_scr.shape[0], w_ref.shape[1]
    n_d = D // LANES
    n_a = WA // LANES
    E = wr_ref.shape[0] // 2
    a = [oa_ref[0, c].astype(F32) for c in range(n_a)]
    rs_a = _inv_rms(a, WA)
    for c in range(n_a):
        cols = slice(c * LANES, (c + 1) * LANES)
        mix_scr[:, cols] = (a[c] * rs_a * ga_ref[:, cols]).astype(BF16)
    b = ob_ref[0].astype(F32)
    rs_b = lax.rsqrt(jnp.mean(b * b, axis=-1, keepdims=True) + RMS_EPS)
    mix_scr[:, WA:] = (b * rs_b * gb_ref[...]).astype(BF16)
    y = jnp.dot(mix_scr[...], w_ref[...], preferred_element_type=F32)
    xs = [xj + y[:, j * LANES:(j + 1) * LANES]
          for j, xj in enumerate(_row_pieces(x_ref, tm, n_d, token_tiled))]
    for j, xj in enumerate(xs):
        x_out_ref[0, pl.ds(j, tm, stride=n_d), :] = xj
    rs = _inv_rms(xs, D)
    h2 = jnp.concatenate([xj * rs * gf_ref[:, j * LANES:(j + 1) * LANES]
                          for j, xj in enumerate(xs)], axis=1)
    h_hi = h2.astype(BF16)
    h_lo = (h2 - h_hi.astype(F32)).astype(BF16)
    p = lax.dot_general(wr_ref[...], h_hi, _NT, preferred_element_type=F32)
    logits = p[:E] + p[E:] + lax.dot_general(wr_ref[:E, :], h_lo, _NT, preferred_element_type=F32)
    z = jnp.exp(logits - jnp.max(logits, axis=0, keepdims=True))
    aff_ref[0] = z / jnp.sum(z, axis=0, keepdims=True)


def _outproj(oa, ob, x, ga, gb, w, gf, w_router, *, S, tm, token_tiled):
    B = x.shape[0]
    D = w.shape[1]
    n_s = S // tm
    n_a = WA // LANES
    E = w_router.shape[1]
    wr_hi = w_router.T.astype(BF16)
    wr_lo = (w_router.T - wr_hi.astype(F32)).astype(BF16)
    wr_t = jnp.concatenate([wr_hi, wr_lo], axis=0)
    const = lambda shape: pl.BlockSpec(shape, lambda i: (0,) * len(shape))
    row = lambda width: pl.BlockSpec((1, tm, width), lambda i: (i // n_s, i % n_s, 0))
    tiled = pl.BlockSpec((1, tm * D // LANES, LANES), lambda i: (i // n_s, i % n_s, 0))
    return pl.pallas_call(
        functools.partial(_outproj_kernel, token_tiled=token_tiled),
        grid=(B * n_s,),
        in_specs=[pl.BlockSpec((1, n_a, tm, LANES), lambda i: (i // n_s, 0, i % n_s, 0)),
                  row(WB_Q), tiled if token_tiled else row(D), const((1, WA)), const((1, WB_Q)), const(w.shape),
                  const((1, D)), const(wr_t.shape)],
        out_specs=[tiled, pl.BlockSpec((1, E, tm), lambda i: (i // n_s, 0, i % n_s))],
        out_shape=[jax.ShapeDtypeStruct((B, S * D // LANES, LANES), F32),
                   jax.ShapeDtypeStruct((B, E, S), F32)],
        scratch_shapes=[pltpu.VMEM((tm, WA + WB_Q), BF16)],
        compiler_params=pltpu.CompilerParams(dimension_semantics=("arbitrary",),
                                             vmem_limit_bytes=VMEM_LIMIT),
        name="outproj",
    )(oa, ob, x, ga.reshape(1, WA), gb.reshape(1, WB_Q), w, gf.reshape(1, D), wr_t)


def _select_kernel(aff_ref, tri_ref, idx_ref, cnt_scr, *, C):
    aff = aff_ref[0]
    E, S = aff.shape
    n_t = S // LANES
    bits = pltpu.bitcast(aff, jnp.int32)

    def search(i, t):
        cand = t | jnp.left_shift(jnp.int32(1), 30 - i)
        n_ge = jnp.sum((bits >= cand).astype(F32), axis=1, keepdims=True)
        return jnp.where(n_ge >= C, cand, t)

    t = lax.fori_loop(0, 31, search, jnp.zeros((E, 1), jnp.int32))
    above = bits > t
    tied = bits == t
    n_ties = C - jnp.sum(above.astype(F32), axis=1, keepdims=True)

    def running_count(mask):
        stacked = jnp.concatenate([mask[:, k * LANES:(k + 1) * LANES] for k in range(n_t)], axis=0)
        inside = jnp.dot(stacked.astype(BF16), tri_ref[...], preferred_element_type=F32)
        out, carry = [], jnp.zeros((E, 1), F32)
        for k in range(n_t):
            tile = inside[k * E:(k + 1) * E] + carry
            out.append(tile)
            carry = tile[:, LANES - 1:]
        return jnp.concatenate(out, axis=1)

    chosen = above | (tied & (running_count(tied) <= n_ties))
    cnt_scr[...] = running_count(chosen)
    c_col = lax.broadcasted_iota(jnp.int32, (C, 1), 0).astype(F32)
    lane = lax.broadcasted_iota(jnp.int32, (1, LANES), 1)

    def invert(e, acc):
        n_before = jnp.sum((cnt_scr[pl.ds(e, 1), :] <= c_col).astype(F32), axis=1, keepdims=True)
        return jnp.where(lane == e, n_before, acc)

    table = lax.fori_loop(0, E, invert, jnp.zeros((C, LANES), F32))
    idx_ref[0] = table.T[:E].astype(jnp.int32)


def _select(aff, C):
    B, E, S = aff.shape
    tri = (jnp.arange(LANES)[:, None] <= jnp.arange(LANES)[None, :]).astype(BF16)
    return pl.pallas_call(
        functools.partial(_select_kernel, C=C),
        grid=(B,),
        in_specs=[pl.BlockSpec((1, E, S), lambda b: (b, 0, 0)),
                  pl.BlockSpec((LANES, LANES), lambda b: (0, 0))],
        out_specs=pl.BlockSpec((1, E, C), lambda b: (b, 0, 0)),
        out_shape=jax.ShapeDtypeStruct((B, E, C), jnp.int32),
        scratch_shapes=[pltpu.VMEM((E, S), F32)],
        compiler_params=pltpu.CompilerParams(dimension_semantics=("arbitrary",),
                                             vmem_limit_bytes=VMEM_LIMIT),
        name="select",
    )(aff, tri)


GATHER_GROUP = 16
SCATTER_GROUP = 8


def _moe_kernel(idx_ref, idx_prev_ref, idx_next_ref, aff_ref, aff_prev_ref, x_hbm, gf_ref,
                wg_ref, wu_ref, wd_ref, out_hbm, big, gt, yt, lhs, sem_in, sem_out, *, n_d):
    b, e = pl.program_id(0), pl.program_id(1)
    n_b, n_e = pl.num_programs(0), pl.num_programs(1)
    C, D = lhs.shape
    par = e % 2
    X, ACC = 0, 1

    def load(row):
        return pltpu.make_async_copy(x_hbm.at[row], big.at[X], sem_in)

    def store(row):
        return pltpu.make_async_copy(big.at[ACC], out_hbm.at[row], sem_out)

    def token_rows(t):
        return pl.ds(pl.multiple_of(t * n_d, n_d), n_d)

    def scatter_group(c0, idx, aff, y_par):
        updates = []
        for i in range(SCATTER_GROUP):
            t = idx[0, 0, c0 + i]
            rows = token_rows(t)
            updates.append((rows, big[ACC, rows, :] + yt[y_par, token_rows(c0 + i), :] * aff[0, 0, t]))
        for rows, val in updates:
            big[ACC, rows, :] = val

    @pl.when((b == 0) & (e == 0))
    def _():
        load(0).start()

    @pl.when(e == 0)
    def _():
        load(b).wait()

        def gather(g, carry):
            for i in range(GATHER_GROUP):
                c = g * GATHER_GROUP + i
                gt[0, token_rows(c), :] = big[X, token_rows(idx_ref[0, 0, c]), :]
            return carry

        lax.fori_loop(0, C // GATHER_GROUP, gather, 0)

    @pl.when((e == n_e - 1) & (b + 1 < n_b))
    def _():
        load(b + 1).start()

    def step(with_prev):
        xs = [gt[par, pl.ds(j, C, stride=n_d), :] for j in range(n_d)]
        rs = _inv_rms(xs, D)
        for j in range(n_d):
            cols = slice(j * LANES, (j + 1) * LANES)
            lhs[:, cols] = (gt[par, pl.ds(j, C, stride=n_d), :] * rs * gf_ref[:, cols]).astype(BF16)
        src = jnp.where(e == n_e - 1, ACC, X)
        h = lhs[...]
        pieces = n_d // 2
        row_blocks = 4
        rb = C // row_blocks
        share = C // (pieces * 2 * row_blocks)
        slot = 0
        for n in range(pieces):
            fcols = slice(2 * n * LANES, (2 * n + 2) * LANES)
            a = jnp.dot(h, wg_ref[0, :, fcols], preferred_element_type=F32)
            u = jnp.dot(h, wu_ref[0, :, fcols], preferred_element_type=F32)
            act = a / (1.0 + jnp.exp(-a)) * u
            for r in range(row_blocks):
                for jj in range(2):
                    base = (2 * n + jj) * C + r * rb
                    gt[par, base:base + rb, :] = act[r * rb:(r + 1) * rb, jj * LANES:(jj + 1) * LANES]
                    for c in range(slot * share, (slot + 1) * share):
                        gt[1 - par, token_rows(c), :] = big[src, token_rows(idx_next_ref[0, 0, c]), :]
                    slot += 1
        hid = jnp.concatenate([gt[par, j * C:(j + 1) * C, :].astype(BF16) for j in range(n_d)], axis=1)
        slot = 0
        for n in range(pieces):
            ncols = slice(2 * n * LANES, (2 * n + 2) * LANES)
            part = jnp.dot(hid, wd_ref[0, :, ncols], preferred_element_type=F32)
            for r in range(row_blocks):
                for jj in range(2):
                    if with_prev:
                        for c0 in range(slot * share, (slot + 1) * share, SCATTER_GROUP):
                            scatter_group(c0, idx_prev_ref, aff_prev_ref, 1 - par)
                    slot += 1
                    yt[par, pl.ds(r * rb * n_d + 2 * n + jj, rb, stride=n_d), :] = (
                        part[r * rb:(r + 1) * rb, jj * LANES:(jj + 1) * LANES])

    @pl.when(e == 0)
    def _():
        step(False)

        @pl.when(b > 0)
        def _():
            store(b - 1).wait()

        big[ACC] = big[X]

    @pl.when(e > 0)
    def _():
        step(True)

    @pl.when(e == n_e - 1)
    def _():
        def scatter(g, carry):
            scatter_group(g * SCATTER_GROUP, idx_ref, aff_ref, par)
            return carry

        lax.fori_loop(0, C // SCATTER_GROUP, scatter, 0)
        store(b).start()

    @pl.when((e == n_e - 1) & (b == n_b - 1))
    def _():
        store(b).wait()


def _moe(x_tt, idx, aff, gf, wg, wu, wd):
    B, rows, _ = x_tt.shape
    _, E, C = idx.shape
    S = aff.shape[2]
    D, Fd = wg.shape[1], wg.shape[2]
    n_d = D // LANES
    assert E % 2 == 0 and C % GATHER_GROUP == 0 and C % (4 * n_d * SCATTER_GROUP) == 0
    assert Fd == D
    last = B * E - 1

    def smem(n, shift):
        return pl.BlockSpec((1, 1, n), lambda b, e: (jnp.clip(b * E + e + shift, 0, last), 0, 0),
                            memory_space=pltpu.SMEM)

    weight = lambda shape: pl.BlockSpec((1,) + shape, lambda b, e: (e, 0, 0))
    idx3, aff3 = idx.reshape(B * E, 1, C), aff.reshape(B * E, 1, S)
    return pl.pallas_call(
        functools.partial(_moe_kernel, n_d=n_d),
        grid=(B, E),
        in_specs=[smem(C, 0), smem(C, -1), smem(C, 1), smem(S, 0), smem(S, -1),
                  pl.BlockSpec(memory_space=pl.ANY),
                  pl.BlockSpec((1, D), lambda b, e: (0, 0)),
                  weight((D, Fd)), weight((D, Fd)), weight((Fd, D))],
        out_specs=pl.BlockSpec(memory_space=pl.ANY),
        out_shape=jax.ShapeDtypeStruct(x_tt.shape, F32),
        scratch_shapes=[pltpu.VMEM((2, rows, LANES), F32),
                        pltpu.VMEM((2, C * n_d, LANES), F32), pltpu.VMEM((2, C * n_d, LANES), F32),
                        pltpu.VMEM((C, D), BF16),
                        pltpu.SemaphoreType.DMA, pltpu.SemaphoreType.DMA],
        compiler_params=pltpu.CompilerParams(dimension_semantics=("arbitrary", "arbitrary"),
                                             vmem_limit_bytes=MOE_VMEM_LIMIT),
        name="moe",
    )(idx3, idx3, idx3, aff3, aff3, x_tt, gf.reshape(1, D), wg, wu, wd)


def _final_norm_kernel(x_ref, g_ref, o_ref):
    _, tm, D = o_ref.shape
    xs = _row_pieces(x_ref, tm, D // LANES, True)
    rs = _inv_rms(xs, D)
    for j, xj in enumerate(xs):
        cols = slice(j * LANES, (j + 1) * LANES)
        o_ref[0, :, cols] = xj * rs * g_ref[:, cols]


def _final_norm(x_tt, g, *, S, tm):
    B = x_tt.shape[0]
    D = g.shape[0]
    n_s = S // tm
    row = pl.BlockSpec((1, tm, D), lambda i: (i // n_s, i % n_s, 0))
    return pl.pallas_call(
        _final_norm_kernel,
        grid=(B * n_s,),
        in_specs=[pl.BlockSpec((1, tm * D // LANES, LANES), lambda i: (i // n_s, i % n_s, 0)),
                  pl.BlockSpec((1, D), lambda i: (0, 0))],
        out_specs=row,
        out_shape=jax.ShapeDtypeStruct((B, S, D), F32),
        compiler_params=pltpu.CompilerParams(dimension_semantics=("arbitrary",)),
        name="final_norm",
    )(x_tt, g.reshape(1, D))


def _rope_tables(S):
    inv = 1.0 / (ROPE_THETA ** (jnp.arange(0, HEAD_DIM, 2, dtype=F32) / HEAD_DIM))
    ang = jnp.arange(S, dtype=F32)[:, None] * inv[None, :]
    cos, sin = jnp.cos(ang), jnp.sin(ang)
    reps = LANES // HEAD_DIM
    cos_t = jnp.tile(jnp.concatenate([cos, cos], axis=-1), (1, reps))
    sin_t = jnp.tile(jnp.concatenate([-sin, sin], axis=-1), (1, reps))
    return cos_t, sin_t


def kernel(x, w_in, w_out, g_attn, g_mix_a, g_mix_b, sink, g_ffn, w_router, w_gate, w_up, w_down, g_final):
    B, S, D = x.shape
    depth = w_in.shape[0]
    tm = min(512, S)
    C = min(CAPACITY_FACTOR * S // N_EXPERTS, S)
    cos_t, sin_t = _rope_tables(S)
    for l in range(depth):
        tt = l > 0
        qa, ka, va, qb, kb, vb = _inproj(x, g_attn[l], w_in[l].astype(BF16), cos_t, sin_t,
                                         S=S, tm=tm, token_tiled=tt)
        oa = _attn_a(qa, ka, va)
        ob = _attn_b(qb, kb, vb, sink[l])
        x, aff = _outproj(oa, ob, x, g_mix_a[l], g_mix_b[l], w_out[l].astype(BF16),
                          g_ffn[l], w_router[l], S=S, tm=tm, token_tiled=tt)
        x = _moe(x, _select(aff, C), aff, g_ffn[l], w_gate[l].astype(BF16), w_up[l].astype(BF16),
                 w_down[l].astype(BF16))
    return _final_norm(x, g_final, S=S, tm=tm)
```

```python
import functools

import jax
import jax.numpy as jnp
from jax import lax
from jax.experimental import pallas as pl
from jax.experimental.pallas import tpu as pltpu

HEAD_DIM = 64
LANES = 128
N_HEADS_A = 8
DILATIONS = ((128, 1), (512, 4), (2048, 16))
N_HEADS_B = 8
N_KV_B = 2
WINDOW_B = 128
WA = N_HEADS_A * HEAD_DIM
WB_Q = N_HEADS_B * HEAD_DIM
WB_KV = N_KV_B * HEAD_DIM
ROPE_THETA = 10000.0
N_EXPERTS = 16
CAPACITY_FACTOR = 2
RMS_EPS = 1e-6
NEG_INF = -1e30
LOG2E = 1.4426950408889634
VMEM_LIMIT = 56 * 1024 * 1024
MOE_VMEM_LIMIT = 60 * 1024 * 1024

F32 = jnp.float32
BF16 = jnp.bfloat16
_NT = (((1,), (1,)), ((), ()))


def _row_pieces(x_ref, tm, n, token_tiled, start=0):
    if token_tiled:
        return [x_ref[0, pl.ds(start * n + j, tm, stride=n), :] for j in range(n)]
    return [x_ref[0, start:start + tm, j * LANES:(j + 1) * LANES] for j in range(n)]


def _inv_rms(pieces, width):
    sq = pieces[0] * pieces[0]
    for p in pieces[1:]:
        sq = sq + p * p
    return lax.rsqrt(jnp.sum(sq, axis=-1, keepdims=True) / width + RMS_EPS)


def _lower_half_mask():
    return lax.broadcasted_iota(jnp.int32, (1, LANES), 1) < HEAD_DIM


def _inproj_kernel(x_ref, g_ref, w_ref, cos_ref, sin_ref,
                   qa_ref, ka_ref, va_ref, qb_ref, kb_ref, vb_ref, h_scr, *, token_tiled):
    tm, D = h_scr.shape
    xs = _row_pieces(x_ref, tm, D // LANES, token_tiled)
    rs = _inv_rms(xs, D)
    for j, xj in enumerate(xs):
        cols = slice(j * LANES, (j + 1) * LANES)
        h_scr[:, cols] = (xj * rs * g_ref[:, cols]).astype(BF16)
    cos = cos_ref[...]
    sin = sin_ref[...]
    lane = lax.broadcasted_iota(jnp.int32, (1, LANES), 1)
    first = (lane % HEAD_DIM) < (HEAD_DIM // 2)

    def rope(t, scale):
        partner = jnp.where(first, pltpu.roll(t, LANES - HEAD_DIM // 2, 1),
                            pltpu.roll(t, HEAD_DIM // 2, 1))
        return (t * cos + partner * sin) * scale

    q_scale = HEAD_DIM ** -0.5 * LOG2E
    n_a = WA // LANES
    n_b = WB_Q // LANES
    plan = (
        (qa_ref, True, 0 * n_a, n_a, True, q_scale),
        (ka_ref, True, 1 * n_a, n_a, True, 1.0),
        (va_ref, True, 2 * n_a, n_a, False, 1.0),
        (qb_ref, False, 3 * n_a, n_b, True, q_scale),
        (kb_ref, False, 3 * n_a + n_b, 1, True, 1.0),
        (vb_ref, False, 3 * n_a + n_b + 1, 1, False, 1.0),
    )
    h = h_scr[...]
    for ref, tile_major, start, count, do_rope, scale in plan:
        for c0 in range(0, count, 2):
            width = min(2, count - c0)
            col = (start + c0) * LANES
            t = jnp.dot(h, w_ref[:, col:col + width * LANES], preferred_element_type=F32)
            for j in range(width):
                tj = t[:, j * LANES:(j + 1) * LANES]
                val = (rope(tj, scale) if do_rope else tj).astype(BF16)
                if tile_major:
                    ref[0, c0 + j] = val
                else:
                    ref[0, :, (c0 + j) * LANES:(c0 + j + 1) * LANES] = val


def _inproj(x, g, w, cos, sin, *, S, tm, token_tiled):
    B = x.shape[0]
    D = w.shape[0]
    n_s = S // tm
    n_a = WA // LANES
    a_shape = jax.ShapeDtypeStruct((B, n_a, S, LANES), BF16)
    a_spec = pl.BlockSpec((1, n_a, tm, LANES), lambda i: (i // n_s, 0, i % n_s, 0))

    def row_spec(width):
        return pl.BlockSpec((1, tm, width), lambda i: (i // n_s, i % n_s, 0))

    tab_spec = pl.BlockSpec((tm, LANES), lambda i: (i % n_s, 0))
    x_spec = (pl.BlockSpec((1, tm * D // LANES, LANES), lambda i: (i // n_s, i % n_s, 0))
              if token_tiled else row_spec(D))
    return pl.pallas_call(
        functools.partial(_inproj_kernel, token_tiled=token_tiled),
        grid=(B * n_s,),
        in_specs=[x_spec,
                  pl.BlockSpec((1, D), lambda i: (0, 0)),
                  pl.BlockSpec(w.shape, lambda i: (0, 0)),
                  tab_spec, tab_spec],
        out_specs=[a_spec, a_spec, a_spec, row_spec(WB_Q), row_spec(WB_KV), row_spec(WB_KV)],
        out_shape=[a_shape, a_shape, a_shape,
                   jax.ShapeDtypeStruct((B, S, WB_Q), BF16),
                   jax.ShapeDtypeStruct((B, S, WB_KV), BF16),
                   jax.ShapeDtypeStruct((B, S, WB_KV), BF16)],
        scratch_shapes=[pltpu.VMEM((tm, D), BF16)],
        compiler_params=pltpu.CompilerParams(dimension_semantics=("arbitrary",),
                                             vmem_limit_bytes=VMEM_LIMIT),
        name="inproj",
    )(x, g.reshape(1, D), w, cos, sin)


def _band_bias(tq, kw, hw):
    row = jnp.arange(2 * tq)[None, :, None] % tq
    col = jnp.arange(kw)[None, None, :]
    off = (jnp.arange(3) * hw)[:, None, None]
    return jnp.where(jnp.abs(row - col + off) <= hw, 0.0, NEG_INF).astype(F32)


def _band_window(q0, hw, L, kw):
    ks = jnp.clip(q0 - hw, 0, L - kw)
    case = jnp.where(q0 - hw < 0, 0, jnp.where(q0 - hw > L - kw, 2, 1))
    return ks, case


def _band_block(q, k, v, bias, lo):
    tq = q.shape[0]
    zero = jnp.zeros_like(q)
    lhs = jnp.concatenate([jnp.where(lo, q, zero), jnp.where(lo, zero, q)], axis=0)
    s = lax.dot_general(lhs, k, _NT, preferred_element_type=F32) + bias
    m = jnp.max(s, axis=1, keepdims=True)
    e = jnp.exp2(s - m).astype(BF16)
    v_ones = jnp.concatenate([v, jnp.ones_like(v)], axis=1)
    pv = jnp.dot(e, v_ones, preferred_element_type=F32)
    acc2, l2 = pv[:, :LANES], pv[:, LANES:]
    acc = jnp.where(lo, acc2[:tq], acc2[tq:])
    m_t = jnp.where(lo, m[:tq], m[tq:])
    l_t = jnp.where(lo, l2[:tq], l2[tq:])
    return acc, m_t, l_t


def _pattern_geometry(S, tq):
    geo = []
    for window, d in DILATIONS:
        L = S // d
        hw = (window // 2) // d
        geo.append((d, L, hw, min(L, tq + 2 * hw), L // tq))
    return geo


def _attn_a_kernel(q_ref, k_ref, v_ref, bias_ref, o_ref, tmp, tmp2, qd, kd, vd, acc_s, m_s, l_s, *, S, tq):
    lo = _lower_half_mask()
    geo = _pattern_geometry(S, tq)
    for src, dst in ((q_ref, qd), (k_ref, kd), (v_ref, vd)):
        tmp[...] = src[0, 0].astype(F32)
        prev, nxt, d_prev, L_prev = tmp, tmp2, 1, S
        for p, (d, L, _, _, _) in enumerate(geo):
            if d == 1:
                continue
            keep_f32 = p + 1 < len(geo)

            def regroup(r, carry, dst=dst, p=p, L=L, q=d // d_prev, d_prev=d_prev, L_prev=L_prev,
                        prev=prev, nxt=nxt, keep_f32=keep_f32):
                rows = prev[pl.ds((r % d_prev) * L_prev + r // d_prev, L, stride=q), :]
                out = pl.ds(pl.multiple_of(r * L, L), L)
                dst[p - 1, out, :] = rows.astype(BF16)
                if keep_f32:
                    nxt[out, :] = rows
                return carry

            lax.fori_loop(0, d, regroup, 0)
            prev, nxt, d_prev, L_prev = nxt, prev, d, L

    def body(j, carry):
        for p, (d, L, hw, kw, n_qb) in enumerate(geo):
            r = j // n_qb
            q0 = (j % n_qb) * tq
            ks, case = _band_window(q0, hw, L, kw)
            if d == 1:
                rows_q = pl.ds(pl.multiple_of(q0, tq), tq)
                rows_k = pl.ds(pl.multiple_of(ks, hw), kw)
                q, k, v = q_ref[0, 0, rows_q, :], k_ref[0, 0, rows_k, :], v_ref[0, 0, rows_k, :]
            else:
                rows_q = pl.ds(r + d * q0, tq, stride=d)
                dense_q = pl.ds(pl.multiple_of(r * L + q0, tq), tq)
                dense_k = pl.ds(pl.multiple_of(r * L + ks, hw), kw)
                q, k, v = qd[p - 1, dense_q, :], kd[p - 1, dense_k, :], vd[p - 1, dense_k, :]
            acc, m_t, l_t = _band_block(q, k, v, bias_ref[case], lo)
            acc_s[p, rows_q, :] = acc
            m_s[p, rows_q, :] = m_t
            l_s[p, rows_q, :] = l_t
        return carry

    lax.fori_loop(0, S // tq, body, 0, unroll=8)

    def merge(j, carry):
        rows = pl.ds(pl.multiple_of(j * tq, tq), tq)
        ms = [m_s[p, rows, :] for p in range(len(geo))]
        m = functools.reduce(jnp.maximum, ms)
        num = den = None
        for p, mp in enumerate(ms):
            w = jnp.exp2(mp - m)
            num = acc_s[p, rows, :] * w if num is None else num + acc_s[p, rows, :] * w
            den = l_s[p, rows, :] * w if den is None else den + l_s[p, rows, :] * w
        o_ref[0, 0, rows, :] = (num / den).astype(BF16)
        return carry

    lax.fori_loop(0, S // tq, merge, 0, unroll=2)


def _attn_a(qa, ka, va, *, tq=128):
    B, n_a, S, _ = qa.shape
    geo = _pattern_geometry(S, tq)
    _, _, hw, kw, _ = geo[0]
    assert all(g[2] == hw and g[3] == kw and g[1] % tq == 0 for g in geo)
    bias = _band_bias(tq, kw, hw)
    spec = pl.BlockSpec((1, 1, S, LANES), lambda b, c: (b, c, 0, 0))
    assert geo[0][0] == 1
    seq = pltpu.VMEM((S, LANES), F32)
    dilated = pltpu.VMEM((len(geo) - 1, S, LANES), BF16)
    state = pltpu.VMEM((len(geo), S, LANES), F32)
    return pl.pallas_call(
        functools.partial(_attn_a_kernel, S=S, tq=tq),
        grid=(B, n_a),
        in_specs=[spec, spec, spec, pl.BlockSpec(bias.shape, lambda b, c: (0, 0, 0))],
        out_specs=spec,
        out_shape=jax.ShapeDtypeStruct((B, n_a, S, LANES), BF16),
        scratch_shapes=[seq, seq, dilated, dilated, dilated, state, state, state],
        compiler_params=pltpu.CompilerParams(dimension_semantics=("arbitrary", "arbitrary"),
                                             vmem_limit_bytes=VMEM_LIMIT),
        name="attn_a",
    )(qa, ka, va, bias)


def _attn_b_kernel(q_ref, k_ref, v_ref, sink_ref, bias_ref, o_ref, kk, vv, *, S, tq):
    lo = _lower_half_mask()
    hw = WINDOW_B
    kw = min(S, tq + 2 * hw)
    for src, dst in ((k_ref, kk), (v_ref, vv)):
        t = src[0].astype(F32)
        sw = pltpu.roll(t, HEAD_DIM, 1)
        dst[0] = jnp.where(lo, t, sw).astype(BF16)
        dst[1] = jnp.where(lo, sw, t).astype(BF16)
    n_c = WB_Q // LANES
    tiles_per_kv = n_c // N_KV_B

    def body(i, carry):
        q0 = pl.multiple_of(i * tq, tq)
        ks, case = _band_window(q0, hw, S, kw)
        rows_q = pl.ds(q0, tq)
        rows_k = pl.ds(pl.multiple_of(ks, tq), kw)
        bias = bias_ref[case]
        for c in range(n_c):
            cols = slice(c * LANES, (c + 1) * LANES)
            j = c // tiles_per_kv
            acc, m_t, l_t = _band_block(q_ref[0, rows_q, cols], kk[j, rows_k, :], vv[j, rows_k, :],
                                        bias, lo)
            sk = sink_ref[c:c + 1, :]
            m2 = jnp.maximum(m_t, sk)
            a = jnp.exp2(m_t - m2)
            den = l_t * a + jnp.exp2(sk - m2)
            o_ref[0, rows_q, cols] = (acc * a / den).astype(BF16)
        return carry

    lax.fori_loop(0, S // tq, body, 0, unroll=4)


def _attn_b(qb, kb, vb, sink, *, tq=128):
    B, S, _ = qb.shape
    n_c = WB_Q // LANES
    kw = min(S, tq + 2 * WINDOW_B)
    assert S % tq == 0 and tq == WINDOW_B
    bias = _band_bias(tq, kw, WINDOW_B)
    sink_tab = jnp.repeat(sink.reshape(n_c, LANES // HEAD_DIM).astype(F32) * LOG2E, HEAD_DIM, axis=1)
    kv_spec = pl.BlockSpec((1, S, WB_KV), lambda b: (b, 0, 0))
    q_spec = pl.BlockSpec((1, S, WB_Q), lambda b: (b, 0, 0))
    return pl.pallas_call(
        functools.partial(_attn_b_kernel, S=S, tq=tq),
        grid=(B,),
        in_specs=[q_spec, kv_spec, kv_spec, pl.BlockSpec((n_c, LANES), lambda b: (0, 0)),
                  pl.BlockSpec(bias.shape, lambda b: (0, 0, 0))],
        out_specs=q_spec,
        out_shape=jax.ShapeDtypeStruct((B, S, WB_Q), BF16),
        scratch_shapes=[pltpu.VMEM((N_KV_B, S, LANES), BF16)] * 2,
        compiler_params=pltpu.CompilerParams(dimension_semantics=("arbitrary",),
                                             vmem_limit_bytes=VMEM_LIMIT),
        name="attn_b",
    )(qb, kb, vb, sink_tab, bias)


def _outproj_kernel(oa_ref, ob_ref, x_ref, ga_ref, gb_ref, w_ref, gf_ref, wr_ref,
                    x_out_ref, aff_ref, mix_scr, *, token_tiled):
    tm, D = mix_scr.shape[0], w_ref.shape[1]
    n_d = D // LANES
    n_a = WA // LANES
    E = wr_ref.shape[0] // 2
    a = [oa_ref[0, c].astype(F32) for c in range(n_a)]
    rs_a = _inv_rms(a, WA)
    for c in range(n_a):
        cols = slice(c * LANES, (c + 1) * LANES)
        mix_scr[:, cols] = (a[c] * rs_a * ga_ref[:, cols]).astype(BF16)
    b = ob_ref[0].astype(F32)
    rs_b = lax.rsqrt(jnp.mean(b * b, axis=-1, keepdims=True) + RMS_EPS)
    mix_scr[:, WA:] = (b * rs_b * gb_ref[...]).astype(BF16)
    y = jnp.dot(mix_scr[...], w_ref[...], preferred_element_type=F32)
    xs = [xj + y[:, j * LANES:(j + 1) * LANES]
          for j, xj in enumerate(_row_pieces(x_ref, tm, n_d, token_tiled))]
    for j, xj in enumerate(xs):
        x_out_ref[0, pl.ds(j, tm, stride=n_d), :] = xj
    rs = _inv_rms(xs, D)
    h2 = jnp.concatenate([xj * rs * gf_ref[:, j * LANES:(j + 1) * LANES]
                          for j, xj in enumerate(xs)], axis=1)
    h_hi = h2.astype(BF16)
    h_lo = (h2 - h_hi.astype(F32)).astype(BF16)
    p = lax.dot_general(wr_ref[...], h_hi, _NT, preferred_element_type=F32)
    logits = p[:E] + p[E:] + lax.dot_general(wr_ref[:E, :], h_lo, _NT, preferred_element_type=F32)
    z = jnp.exp(logits - jnp.max(logits, axis=0, keepdims=True))
    aff_ref[0] = z / jnp.sum(z, axis=0, keepdims=True)


def _outproj(oa, ob, x, ga, gb, w, gf, w_router, *, S, tm, token_tiled):
    B = x.shape[0]
    D = w.shape[1]
    n_s = S // tm
    n_a = WA // LANES
    E = w_router.shape[1]
    wr_hi = w_router.T.astype(BF16)
    wr_lo = (w_router.T - wr_hi.astype(F32)).astype(BF16)
    wr_t = jnp.concatenate([wr_hi, wr_lo], axis=0)
    const = lambda shape: pl.BlockSpec(shape, lambda i: (0,) * len(shape))
    row = lambda width: pl.BlockSpec((1, tm, width), lambda i: (i // n_s, i % n_s, 0))
    tiled = pl.BlockSpec((1, tm * D // LANES, LANES), lambda i: (i // n_s, i % n_s, 0))
    return pl.pallas_call(
        functools.partial(_outproj_kernel, token_tiled=token_tiled),
        grid=(B * n_s,),
        in_specs=[pl.BlockSpec((1, n_a, tm, LANES), lambda i: (i // n_s, 0, i % n_s, 0)),
                  row(WB_Q), tiled if token_tiled else row(D), const((1, WA)), const((1, WB_Q)), const(w.shape),
                  const((1, D)), const(wr_t.shape)],
        out_specs=[tiled, pl.BlockSpec((1, E, tm), lambda i: (i // n_s, 0, i % n_s))],
        out_shape=[jax.ShapeDtypeStruct((B, S * D // LANES, LANES), F32),
                   jax.ShapeDtypeStruct((B, E, S), F32)],
        scratch_shapes=[pltpu.VMEM((tm, WA + WB_Q), BF16)],
        compiler_params=pltpu.CompilerParams(dimension_semantics=("arbitrary",),
                                             vmem_limit_bytes=VMEM_LIMIT),
        name="outproj",
    )(oa, ob, x, ga.reshape(1, WA), gb.reshape(1, WB_Q), w, gf.reshape(1, D), wr_t)


def _select_kernel(aff_ref, tri_ref, idx_ref, cnt_scr, *, C):
    aff = aff_ref[0]
    E, S = aff.shape
    n_t = S // LANES
    bits = pltpu.bitcast(aff, jnp.int32)

    def search(i, t):
        cand = t | jnp.left_shift(jnp.int32(1), 30 - i)
        n_ge = jnp.sum((bits >= cand).astype(F32), axis=1, keepdims=True)
        return jnp.where(n_ge >= C, cand, t)

    t = lax.fori_loop(0, 31, search, jnp.zeros((E, 1), jnp.int32))
    above = bits > t
    tied = bits == t
    n_ties = C - jnp.sum(above.astype(F32), axis=1, keepdims=True)

    def running_count(mask):
        stacked = jnp.concatenate([mask[:, k * LANES:(k + 1) * LANES] for k in range(n_t)], axis=0)
        both = jnp.dot(stacked.astype(BF16), tri_ref[...], preferred_element_type=F32)
        out, carry = [], jnp.zeros((E, LANES), F32)
        for k in range(n_t):
            out.append(both[k * E:(k + 1) * E, :LANES] + carry)
            carry = carry + both[k * E:(k + 1) * E, LANES:]
        return jnp.concatenate(out, axis=1)

    chosen = above | (tied & (running_count(tied) <= n_ties))
    cnt_scr[...] = running_count(chosen)
    lane = lax.broadcasted_iota(jnp.int32, (1, LANES), 1)
    c_blk = lax.broadcasted_iota(jnp.int32, (LANES, 1), 0).astype(F32).astype(BF16)
    one, zero = jnp.ones((), BF16), jnp.zeros((), BF16)

    def invert(e, acc):
        row = cnt_scr[pl.ds(e, 1), :]
        cols = []
        for blk in range(C // LANES):
            local = jnp.clip(row - blk * LANES, -1.0, LANES + 1.0).astype(BF16)
            part = None
            for k in range(n_t):
                hit = jnp.where(local[:, k * LANES:(k + 1) * LANES] <= c_blk, one, zero)
                part = hit if part is None else part + hit
            cols.append(jnp.sum(part.astype(F32), axis=1, keepdims=True))
        n_before = jnp.concatenate(cols, axis=0)
        return jnp.where(lane == e, n_before, acc)

    table = lax.fori_loop(0, E, invert, jnp.zeros((C, LANES), F32))
    idx_ref[0] = table.T[:E].astype(jnp.int32)


def _select(aff, C):
    B, E, S = aff.shape
    tri = (jnp.arange(LANES)[:, None] <= jnp.arange(LANES)[None, :]).astype(BF16)
    tri = jnp.concatenate([tri, jnp.ones_like(tri)], axis=1)
    return pl.pallas_call(
        functools.partial(_select_kernel, C=C),
        grid=(B,),
        in_specs=[pl.BlockSpec((1, E, S), lambda b: (b, 0, 0)),
                  pl.BlockSpec((LANES, 2 * LANES), lambda b: (0, 0))],
        out_specs=pl.BlockSpec((1, E, C), lambda b: (b, 0, 0)),
        out_shape=jax.ShapeDtypeStruct((B, E, C), jnp.int32),
        scratch_shapes=[pltpu.VMEM((E, S), F32)],
        compiler_params=pltpu.CompilerParams(dimension_semantics=("arbitrary",),
                                             vmem_limit_bytes=VMEM_LIMIT),
        name="select",
    )(aff, tri)


GATHER_GROUP = 16
SCATTER_GROUP = 8


def _moe_kernel(idx_ref, idx_prev_ref, idx_next_ref, aff_ref, aff_prev_ref, x_hbm, gf_ref,
                wg_ref, wu_ref, wd_ref, out_hbm, big, gt, yt, lhs, sem_in, sem_out, *, n_d):
    b, e = pl.program_id(0), pl.program_id(1)
    n_b, n_e = pl.num_programs(0), pl.num_programs(1)
    C, D = lhs.shape
    par = e % 2
    X, ACC = 0, 1

    def load(row):
        return pltpu.make_async_copy(x_hbm.at[row], big.at[X], sem_in)

    def store(row):
        return pltpu.make_async_copy(big.at[ACC], out_hbm.at[row], sem_out)

    def token_rows(t):
        return pl.ds(pl.multiple_of(t * n_d, n_d), n_d)

    def scatter_group(c0, idx, aff, y_par):
        updates = []
        for i in range(SCATTER_GROUP):
            t = idx[0, 0, c0 + i]
            rows = token_rows(t)
            updates.append((rows, big[ACC, rows, :] + yt[y_par, token_rows(c0 + i), :] * aff[0, 0, t]))
        for rows, val in updates:
            big[ACC, rows, :] = val

    @pl.when((b == 0) & (e == 0))
    def _():
        load(0).start()

    @pl.when(e == 0)
    def _():
        load(b).wait()

        def gather(g, carry):
            for i in range(GATHER_GROUP):
                c = g * GATHER_GROUP + i
                gt[0, token_rows(c), :] = big[X, token_rows(idx_ref[0, 0, c]), :]
            return carry

        lax.fori_loop(0, C // GATHER_GROUP, gather, 0)

    @pl.when((e == n_e - 1) & (b + 1 < n_b))
    def _():
        load(b + 1).start()

    def step(with_prev):
        xs = [gt[par, pl.ds(j, C, stride=n_d), :] for j in range(n_d)]
        rs = _inv_rms(xs, D)
        for j in range(n_d):
            cols = slice(j * LANES, (j + 1) * LANES)
            lhs[:, cols] = (gt[par, pl.ds(j, C, stride=n_d), :] * rs * gf_ref[:, cols]).astype(BF16)
        src = jnp.where(e == n_e - 1, ACC, X)
        h = lhs[...]
        pieces = n_d // 2
        row_blocks = 4
        rb = C // row_blocks
        share = C // (pieces * 2 * row_blocks)
        slot = 0
        for n in range(pieces):
            fcols = slice(2 * n * LANES, (2 * n + 2) * LANES)
            a = jnp.dot(h, wg_ref[0, 0, :, fcols], preferred_element_type=F32)
            u = jnp.dot(h, wu_ref[0, 0, :, fcols], preferred_element_type=F32)
            act = a / (1.0 + jnp.exp(-a)) * u
            for r in range(row_blocks):
                for jj in range(2):
                    base = (2 * n + jj) * C + r * rb
                    gt[par, base:base + rb, :] = act[r * rb:(r + 1) * rb, jj * LANES:(jj + 1) * LANES]
                    for c in range(slot * share, (slot + 1) * share):
                        gt[1 - par, token_rows(c), :] = big[src, token_rows(idx_next_ref[0, 0, c]), :]
                    slot += 1
        hid = jnp.concatenate([gt[par, j * C:(j + 1) * C, :].astype(BF16) for j in range(n_d)], axis=1)
        slot = 0
        for n in range(pieces):
            ncols = slice(2 * n * LANES, (2 * n + 2) * LANES)
            part = jnp.dot(hid, wd_ref[0, 0, :, ncols], preferred_element_type=F32)
            for r in range(row_blocks):
                for jj in range(2):
                    if with_prev:
                        for c0 in range(slot * share, (slot + 1) * share, SCATTER_GROUP):
                            scatter_group(c0, idx_prev_ref, aff_prev_ref, 1 - par)
                    slot += 1
                    yt[par, pl.ds(r * rb * n_d + 2 * n + jj, rb, stride=n_d), :] = (
                        part[r * rb:(r + 1) * rb, jj * LANES:(jj + 1) * LANES])

    @pl.when(e == 0)
    def _():
        step(False)

        @pl.when(b > 0)
        def _():
            store(b - 1).wait()

        big[ACC] = big[X]

    @pl.when(e > 0)
    def _():
        step(True)

    @pl.when(e == n_e - 1)
    def _():
        def scatter(g, carry):
            scatter_group(g * SCATTER_GROUP, idx_ref, aff_ref, par)
            return carry

        lax.fori_loop(0, C // SCATTER_GROUP, scatter, 0)
        store(b).start()

    @pl.when((e == n_e - 1) & (b == n_b - 1))
    def _():
        store(b).wait()


def _moe(x_tt, idx, aff, gf, wg, wu, wd, layer):
    B, rows, _ = x_tt.shape
    _, E, C = idx.shape
    S = aff.shape[2]
    D, Fd = wg.shape[2], wg.shape[3]
    n_d = D // LANES
    assert E % 2 == 0 and C % GATHER_GROUP == 0 and C % (4 * n_d * SCATTER_GROUP) == 0
    assert Fd == D
    last = B * E - 1

    def smem(n, shift):
        return pl.BlockSpec((1, 1, n), lambda b, e: (jnp.clip(b * E + e + shift, 0, last), 0, 0),
                            memory_space=pltpu.SMEM)

    weight = lambda shape: pl.BlockSpec((1, 1) + shape, lambda b, e: (layer, e, 0, 0))
    idx3, aff3 = idx.reshape(B * E, 1, C), aff.reshape(B * E, 1, S)
    return pl.pallas_call(
        functools.partial(_moe_kernel, n_d=n_d),
        grid=(B, E),
        in_specs=[smem(C, 0), smem(C, -1), smem(C, 1), smem(S, 0), smem(S, -1),
                  pl.BlockSpec(memory_space=pl.ANY),
                  pl.BlockSpec((1, D), lambda b, e: (0, 0)),
                  weight((D, Fd)), weight((D, Fd)), weight((Fd, D))],
        out_specs=pl.BlockSpec(memory_space=pl.ANY),
        out_shape=jax.ShapeDtypeStruct(x_tt.shape, F32),
        scratch_shapes=[pltpu.VMEM((2, rows, LANES), F32),
                        pltpu.VMEM((2, C * n_d, LANES), F32), pltpu.VMEM((2, C * n_d, LANES), F32),
                        pltpu.VMEM((C, D), BF16),
                        pltpu.SemaphoreType.DMA, pltpu.SemaphoreType.DMA],
        compiler_params=pltpu.CompilerParams(dimension_semantics=("arbitrary", "arbitrary"),
                                             vmem_limit_bytes=MOE_VMEM_LIMIT),
        name="moe",
    )(idx3, idx3, idx3, aff3, aff3, x_tt, gf.reshape(1, D), wg, wu, wd)


def _final_norm_kernel(x_ref, g_ref, o_ref):
    _, tm, D = o_ref.shape
    xs = _row_pieces(x_ref, tm, D // LANES, True)
    rs = _inv_rms(xs, D)
    for j, xj in enumerate(xs):
        cols = slice(j * LANES, (j + 1) * LANES)
        o_ref[0, :, cols] = xj * rs * g_ref[:, cols]


def _final_norm(x_tt, g, *, S, tm):
    B = x_tt.shape[0]
    D = g.shape[0]
    n_s = S // tm
    row = pl.BlockSpec((1, tm, D), lambda i: (i // n_s, i % n_s, 0))
    return pl.pallas_call(
        _final_norm_kernel,
        grid=(B * n_s,),
        in_specs=[pl.BlockSpec((1, tm * D // LANES, LANES), lambda i: (i // n_s, i % n_s, 0)),
                  pl.BlockSpec((1, D), lambda i: (0, 0))],
        out_specs=row,
        out_shape=jax.ShapeDtypeStruct((B, S, D), F32),
        compiler_params=pltpu.CompilerParams(dimension_semantics=("arbitrary",)),
        name="final_norm",
    )(x_tt, g.reshape(1, D))


def _rope_tables(S):
    inv = 1.0 / (ROPE_THETA ** (jnp.arange(0, HEAD_DIM, 2, dtype=F32) / HEAD_DIM))
    ang = jnp.arange(S, dtype=F32)[:, None] * inv[None, :]
    cos, sin = jnp.cos(ang), jnp.sin(ang)
    reps = LANES // HEAD_DIM
    cos_t = jnp.tile(jnp.concatenate([cos, cos], axis=-1), (1, reps))
    sin_t = jnp.tile(jnp.concatenate([-sin, sin], axis=-1), (1, reps))
    return cos_t, sin_t


def kernel(x, w_in, w_out, g_attn, g_mix_a, g_mix_b, sink, g_ffn, w_router, w_gate, w_up, w_down, g_final):
    B, S, D = x.shape
    depth = w_in.shape[0]
    tm = min(512, S)
    C = min(CAPACITY_FACTOR * S // N_EXPERTS, S)
    cos_t, sin_t = _rope_tables(S)
    wg, wu, wd = w_gate.astype(BF16), w_up.astype(BF16), w_down.astype(BF16)
    for l in range(depth):
        tt = l > 0
        qa, ka, va, qb, kb, vb = _inproj(x, g_attn[l], w_in[l].astype(BF16), cos_t, sin_t,
                                         S=S, tm=tm, token_tiled=tt)
        oa = _attn_a(qa, ka, va)
        ob = _attn_b(qb, kb, vb, sink[l])
        x, aff = _outproj(oa, ob, x, g_mix_a[l], g_mix_b[l], w_out[l].astype(BF16),
                          g_ffn[l], w_router[l], S=S, tm=tm, token_tiled=tt)
        x = _moe(x, _select(aff, C), aff, g_ffn[l], wg, wu, wd, l)
    return _final_norm(x, g_final, S=S, tm=tm)
```

```python
import functools

import jax
import jax.numpy as jnp
from jax import lax
from jax.experimental import pallas as pl
from jax.experimental.pallas import tpu as pltpu

HEAD_DIM = 64
LANES = 128
N_HEADS_A = 8
DILATIONS = ((128, 1), (512, 4), (2048, 16))
N_HEADS_B = 8
N_KV_B = 2
WINDOW_B = 128
WA = N_HEADS_A * HEAD_DIM
WB_Q = N_HEADS_B * HEAD_DIM
WB_KV = N_KV_B * HEAD_DIM
ROPE_THETA = 10000.0
N_EXPERTS = 16
CAPACITY_FACTOR = 2
RMS_EPS = 1e-6
NEG_INF = -1e30
LOG2E = 1.4426950408889634
VMEM_LIMIT = 56 * 1024 * 1024
MOE_VMEM_LIMIT = 60 * 1024 * 1024

F32 = jnp.float32
BF16 = jnp.bfloat16
_NT = (((1,), (1,)), ((), ()))


def _row_pieces(x_ref, tm, n, token_tiled, start=0):
    if token_tiled:
        return [x_ref[0, pl.ds(start * n + j, tm, stride=n), :] for j in range(n)]
    return [x_ref[0, start:start + tm, j * LANES:(j + 1) * LANES] for j in range(n)]


def _inv_rms(pieces, width):
    sq = pieces[0] * pieces[0]
    for p in pieces[1:]:
        sq = sq + p * p
    return lax.rsqrt(jnp.sum(sq, axis=-1, keepdims=True) / width + RMS_EPS)


def _lower_half_mask():
    return lax.broadcasted_iota(jnp.int32, (1, LANES), 1) < HEAD_DIM


def _inproj_kernel(x_ref, g_ref, w_ref, cos_ref, sin_ref,
                   qa_ref, ka_ref, va_ref, qb_ref, kb_ref, vb_ref, h_scr, *, token_tiled):
    tm, D = h_scr.shape
    xs = _row_pieces(x_ref, tm, D // LANES, token_tiled)
    rs = _inv_rms(xs, D)
    for j, xj in enumerate(xs):
        cols = slice(j * LANES, (j + 1) * LANES)
        h_scr[:, cols] = (xj * rs * g_ref[:, cols]).astype(BF16)
    cos = cos_ref[...]
    sin = sin_ref[...]
    lane = lax.broadcasted_iota(jnp.int32, (1, LANES), 1)
    first = (lane % HEAD_DIM) < (HEAD_DIM // 2)

    def rope(t, scale):
        partner = jnp.where(first, pltpu.roll(t, LANES - HEAD_DIM // 2, 1),
                            pltpu.roll(t, HEAD_DIM // 2, 1))
        return (t * cos + partner * sin) * scale

    q_scale = HEAD_DIM ** -0.5 * LOG2E
    n_a = WA // LANES
    n_b = WB_Q // LANES
    plan = (
        (qa_ref, True, 0 * n_a, n_a, True, q_scale),
        (ka_ref, True, 1 * n_a, n_a, True, 1.0),
        (va_ref, True, 2 * n_a, n_a, False, 1.0),
        (qb_ref, False, 3 * n_a, n_b, True, q_scale),
        (kb_ref, False, 3 * n_a + n_b, 1, True, 1.0),
        (vb_ref, False, 3 * n_a + n_b + 1, 1, False, 1.0),
    )
    h = h_scr[...]
    for ref, tile_major, start, count, do_rope, scale in plan:
        for c0 in range(0, count, 2):
            width = min(2, count - c0)
            col = (start + c0) * LANES
            t = jnp.dot(h, w_ref[:, col:col + width * LANES], preferred_element_type=F32)
            for j in range(width):
                tj = t[:, j * LANES:(j + 1) * LANES]
                val = (rope(tj, scale) if do_rope else tj).astype(BF16)
                if tile_major:
                    ref[0, c0 + j] = val
                else:
                    ref[0, :, (c0 + j) * LANES:(c0 + j + 1) * LANES] = val


def _inproj(x, g, w, cos, sin, *, S, tm, token_tiled):
    B = x.shape[0]
    D = w.shape[0]
    n_s = S // tm
    n_a = WA // LANES
    a_shape = jax.ShapeDtypeStruct((B, n_a, S, LANES), BF16)
    a_spec = pl.BlockSpec((1, n_a, tm, LANES), lambda i: (i // n_s, 0, i % n_s, 0))

    def row_spec(width):
        return pl.BlockSpec((1, tm, width), lambda i: (i // n_s, i % n_s, 0))

    tab_spec = pl.BlockSpec((tm, LANES), lambda i: (i % n_s, 0))
    x_spec = (pl.BlockSpec((1, tm * D // LANES, LANES), lambda i: (i // n_s, i % n_s, 0))
              if token_tiled else row_spec(D))
    return pl.pallas_call(
        functools.partial(_inproj_kernel, token_tiled=token_tiled),
        grid=(B * n_s,),
        in_specs=[x_spec,
                  pl.BlockSpec((1, D), lambda i: (0, 0)),
                  pl.BlockSpec(w.shape, lambda i: (0, 0)),
                  tab_spec, tab_spec],
        out_specs=[a_spec, a_spec, a_spec, row_spec(WB_Q), row_spec(WB_KV), row_spec(WB_KV)],
        out_shape=[a_shape, a_shape, a_shape,
                   jax.ShapeDtypeStruct((B, S, WB_Q), BF16),
                   jax.ShapeDtypeStruct((B, S, WB_KV), BF16),
                   jax.ShapeDtypeStruct((B, S, WB_KV), BF16)],
        scratch_shapes=[pltpu.VMEM((tm, D), BF16)],
        compiler_params=pltpu.CompilerParams(dimension_semantics=("arbitrary",),
                                             vmem_limit_bytes=VMEM_LIMIT),
        name="inproj",
    )(x, g.reshape(1, D), w, cos, sin)


def _band_bias(tq, kw, hw):
    row = jnp.arange(2 * tq)[None, :, None] % tq
    col = jnp.arange(kw)[None, None, :]
    off = (jnp.arange(3) * hw)[:, None, None]
    return jnp.where(jnp.abs(row - col + off) <= hw, 0.0, NEG_INF).astype(F32)


def _band_window(q0, hw, L, kw):
    ks = jnp.clip(q0 - hw, 0, L - kw)
    case = jnp.where(q0 - hw < 0, 0, jnp.where(q0 - hw > L - kw, 2, 1))
    return ks, case


def _band_block(q, k, v, bias, lo):
    tq = q.shape[0]
    zero = jnp.zeros_like(q)
    lhs = jnp.concatenate([jnp.where(lo, q, zero), jnp.where(lo, zero, q)], axis=0)
    s = lax.dot_general(lhs, k, _NT, preferred_element_type=F32) + bias
    m = jnp.max(s, axis=1, keepdims=True)
    e = jnp.exp2(s - m).astype(BF16)
    v_ones = jnp.concatenate([v, jnp.ones_like(v)], axis=1)
    pv = jnp.dot(e, v_ones, preferred_element_type=F32)
    acc2, l2 = pv[:, :LANES], pv[:, LANES:]
    acc = jnp.where(lo, acc2[:tq], acc2[tq:])
    m_t = jnp.where(lo, m[:tq], m[tq:])
    l_t = jnp.where(lo, l2[:tq], l2[tq:])
    return acc, m_t, l_t


def _pattern_geometry(S, tq):
    geo = []
    for window, d in DILATIONS:
        L = S // d
        hw = (window // 2) // d
        geo.append((d, L, hw, min(L, tq + 2 * hw), L // tq))
    return geo


def _attn_a_kernel(q_ref, k_ref, v_ref, bias_ref, o_ref, tmp, tmp2, qd, kd, vd, acc_s, m_s, l_s, *, S, tq):
    lo = _lower_half_mask()
    geo = _pattern_geometry(S, tq)
    for src, dst in ((q_ref, qd), (k_ref, kd), (v_ref, vd)):
        tmp[...] = src[0, 0].astype(F32)
        prev, nxt, d_prev, L_prev = tmp, tmp2, 1, S
        for p, (d, L, _, _, _) in enumerate(geo):
            if d == 1:
                continue
            keep_f32 = p + 1 < len(geo)

            def regroup(r, carry, dst=dst, p=p, L=L, q=d // d_prev, d_prev=d_prev, L_prev=L_prev,
                        prev=prev, nxt=nxt, keep_f32=keep_f32):
                rows = prev[pl.ds((r % d_prev) * L_prev + r // d_prev, L, stride=q), :]
                out = pl.ds(pl.multiple_of(r * L, L), L)
                dst[p - 1, out, :] = rows.astype(BF16)
                if keep_f32:
                    nxt[out, :] = rows
                return carry

            lax.fori_loop(0, d, regroup, 0)
            prev, nxt, d_prev, L_prev = nxt, prev, d, L

    def body(j, carry):
        for p, (d, L, hw, kw, n_qb) in enumerate(geo):
            r = j // n_qb
            q0 = (j % n_qb) * tq
            ks, case = _band_window(q0, hw, L, kw)
            if d == 1:
                rows_q = pl.ds(pl.multiple_of(q0, tq), tq)
                rows_k = pl.ds(pl.multiple_of(ks, hw), kw)
                q, k, v = q_ref[0, 0, rows_q, :], k_ref[0, 0, rows_k, :], v_ref[0, 0, rows_k, :]
            else:
                rows_q = pl.ds(r + d * q0, tq, stride=d)
                dense_q = pl.ds(pl.multiple_of(r * L + q0, tq), tq)
                dense_k = pl.ds(pl.multiple_of(r * L + ks, hw), kw)
                q, k, v = qd[p - 1, dense_q, :], kd[p - 1, dense_k, :], vd[p - 1, dense_k, :]
            acc, m_t, l_t = _band_block(q, k, v, bias_ref[case], lo)
            acc_s[p, rows_q, :] = acc
            m_s[p, rows_q, :] = m_t
            l_s[p, rows_q, :] = l_t
        return carry

    lax.fori_loop(0, S // tq, body, 0, unroll=8)

    def merge(j, carry):
        rows = pl.ds(pl.multiple_of(j * tq, tq), tq)
        ms = [m_s[p, rows, :] for p in range(len(geo))]
        m = functools.reduce(jnp.maximum, ms)
        num = den = None
        for p, mp in enumerate(ms):
            w = jnp.exp2(mp - m)
            num = acc_s[p, rows, :] * w if num is None else num + acc_s[p, rows, :] * w
            den = l_s[p, rows, :] * w if den is None else den + l_s[p, rows, :] * w
        o_ref[0, 0, rows, :] = (num / den).astype(BF16)
        return carry

    lax.fori_loop(0, S // tq, merge, 0, unroll=2)


def _attn_a(qa, ka, va, *, tq=128):
    B, n_a, S, _ = qa.shape
    geo = _pattern_geometry(S, tq)
    _, _, hw, kw, _ = geo[0]
    assert all(g[2] == hw and g[3] == kw and g[1] % tq == 0 for g in geo)
    bias = _band_bias(tq, kw, hw)
    spec = pl.BlockSpec((1, 1, S, LANES), lambda b, c: (b, c, 0, 0))
    assert geo[0][0] == 1
    seq = pltpu.VMEM((S, LANES), F32)
    dilated = pltpu.VMEM((len(geo) - 1, S, LANES), BF16)
    state = pltpu.VMEM((len(geo), S, LANES), F32)
    return pl.pallas_call(
        functools.partial(_attn_a_kernel, S=S, tq=tq),
        grid=(B, n_a),
        in_specs=[spec, spec, spec, pl.BlockSpec(bias.shape, lambda b, c: (0, 0, 0))],
        out_specs=spec,
        out_shape=jax.ShapeDtypeStruct((B, n_a, S, LANES), BF16),
        scratch_shapes=[seq, seq, dilated, dilated, dilated, state, state, state],
        compiler_params=pltpu.CompilerParams(dimension_semantics=("arbitrary", "arbitrary"),
                                             vmem_limit_bytes=VMEM_LIMIT),
        name="attn_a",
    )(qa, ka, va, bias)


def _attn_b_kernel(q_ref, k_ref, v_ref, sink_ref, bias_ref, o_ref, kk, vv, *, S, tq):
    lo = _lower_half_mask()
    hw = WINDOW_B
    kw = min(S, tq + 2 * hw)
    for src, dst in ((k_ref, kk), (v_ref, vv)):
        t = src[0].astype(F32)
        sw = pltpu.roll(t, HEAD_DIM, 1)
        dst[0] = jnp.where(lo, t, sw).astype(BF16)
        dst[1] = jnp.where(lo, sw, t).astype(BF16)
    n_c = WB_Q // LANES
    tiles_per_kv = n_c // N_KV_B

    def body(i, carry):
        q0 = pl.multiple_of(i * tq, tq)
        ks, case = _band_window(q0, hw, S, kw)
        rows_q = pl.ds(q0, tq)
        rows_k = pl.ds(pl.multiple_of(ks, tq), kw)
        bias = bias_ref[case]
        for c in range(n_c):
            cols = slice(c * LANES, (c + 1) * LANES)
            j = c // tiles_per_kv
            acc, m_t, l_t = _band_block(q_ref[0, rows_q, cols], kk[j, rows_k, :], vv[j, rows_k, :],
                                        bias, lo)
            sk = sink_ref[c:c + 1, :]
            m2 = jnp.maximum(m_t, sk)
            a = jnp.exp2(m_t - m2)
            den = l_t * a + jnp.exp2(sk - m2)
            o_ref[0, rows_q, cols] = (acc * a / den).astype(BF16)
        return carry

    lax.fori_loop(0, S // tq, body, 0, unroll=4)


def _attn_b(qb, kb, vb, sink, *, tq=128):
    B, S, _ = qb.shape
    n_c = WB_Q // LANES
    kw = min(S, tq + 2 * WINDOW_B)
    assert S % tq == 0 and tq == WINDOW_B
    bias = _band_bias(tq, kw, WINDOW_B)
    sink_tab = jnp.repeat(sink.reshape(n_c, LANES // HEAD_DIM).astype(F32) * LOG2E, HEAD_DIM, axis=1)
    kv_spec = pl.BlockSpec((1, S, WB_KV), lambda b: (b, 0, 0))
    q_spec = pl.BlockSpec((1, S, WB_Q), lambda b: (b, 0, 0))
    return pl.pallas_call(
        functools.partial(_attn_b_kernel, S=S, tq=tq),
        grid=(B,),
        in_specs=[q_spec, kv_spec, kv_spec, pl.BlockSpec((n_c, LANES), lambda b: (0, 0)),
                  pl.BlockSpec(bias.shape, lambda b: (0, 0, 0))],
        out_specs=q_spec,
        out_shape=jax.ShapeDtypeStruct((B, S, WB_Q), BF16),
        scratch_shapes=[pltpu.VMEM((N_KV_B, S, LANES), BF16)] * 2,
        compiler_params=pltpu.CompilerParams(dimension_semantics=("arbitrary",),
                                             vmem_limit_bytes=VMEM_LIMIT),
        name="attn_b",
    )(qb, kb, vb, sink_tab, bias)


def _outproj_kernel(oa_ref, ob_ref, x_ref, ga_ref, gb_ref, w_ref, gf_ref, wr_ref,
                    x_out_ref, aff_ref, mix_scr, *, token_tiled):
    tm, D = mix_scr.shape[0], w_ref.shape[1]
    n_d = D // LANES
    n_a = WA // LANES
    E = wr_ref.shape[0] // 2
    a = [oa_ref[0, c].astype(F32) for c in range(n_a)]
    rs_a = _inv_rms(a, WA)
    for c in range(n_a):
        cols = slice(c * LANES, (c + 1) * LANES)
        mix_scr[:, cols] = (a[c] * rs_a * ga_ref[:, cols]).astype(BF16)
    b = ob_ref[0].astype(F32)
    rs_b = lax.rsqrt(jnp.mean(b * b, axis=-1, keepdims=True) + RMS_EPS)
    mix_scr[:, WA:] = (b * rs_b * gb_ref[...]).astype(BF16)
    y = jnp.dot(mix_scr[...], w_ref[...], preferred_element_type=F32)
    xs = [xj + y[:, j * LANES:(j + 1) * LANES]
          for j, xj in enumerate(_row_pieces(x_ref, tm, n_d, token_tiled))]
    for j, xj in enumerate(xs):
        x_out_ref[0, pl.ds(j, tm, stride=n_d), :] = xj
    rs = _inv_rms(xs, D)
    h2 = jnp.concatenate([xj * rs * gf_ref[:, j * LANES:(j + 1) * LANES]
                          for j, xj in enumerate(xs)], axis=1)
    h_hi = h2.astype(BF16)
    h_lo = (h2 - h_hi.astype(F32)).astype(BF16)
    p = lax.dot_general(wr_ref[...], h_hi, _NT, preferred_element_type=F32)
    logits = p[:E] + p[E:] + lax.dot_general(wr_ref[:E, :], h_lo, _NT, preferred_element_type=F32)
    z = jnp.exp(logits - jnp.max(logits, axis=0, keepdims=True))
    aff_ref[0] = z / jnp.sum(z, axis=0, keepdims=True)


def _outproj(oa, ob, x, ga, gb, w, gf, w_router, *, S, tm, token_tiled):
    B = x.shape[0]
    D = w.shape[1]
    n_s = S // tm
    n_a = WA // LANES
    E = w_router.shape[1]
    wr_hi = w_router.T.astype(BF16)
    wr_lo = (w_router.T - wr_hi.astype(F32)).astype(BF16)
    wr_t = jnp.concatenate([wr_hi, wr_lo], axis=0)
    const = lambda shape: pl.BlockSpec(shape, lambda i: (0,) * len(shape))
    row = lambda width: pl.BlockSpec((1, tm, width), lambda i: (i // n_s, i % n_s, 0))
    tiled = pl.BlockSpec((1, tm * D // LANES, LANES), lambda i: (i // n_s, i % n_s, 0))
    return pl.pallas_call(
        functools.partial(_outproj_kernel, token_tiled=token_tiled),
        grid=(B * n_s,),
        in_specs=[pl.BlockSpec((1, n_a, tm, LANES), lambda i: (i // n_s, 0, i % n_s, 0)),
                  row(WB_Q), tiled if token_tiled else row(D), const((1, WA)), const((1, WB_Q)), const(w.shape),
                  const((1, D)), const(wr_t.shape)],
        out_specs=[tiled, pl.BlockSpec((1, E, tm), lambda i: (i // n_s, 0, i % n_s))],
        out_shape=[jax.ShapeDtypeStruct((B, S * D // LANES, LANES), F32),
                   jax.ShapeDtypeStruct((B, E, S), F32)],
        scratch_shapes=[pltpu.VMEM((tm, WA + WB_Q), BF16)],
        compiler_params=pltpu.CompilerParams(dimension_semantics=("arbitrary",),
                                             vmem_limit_bytes=VMEM_LIMIT),
        name="outproj",
    )(oa, ob, x, ga.reshape(1, WA), gb.reshape(1, WB_Q), w, gf.reshape(1, D), wr_t)


def _select_kernel(aff_ref, tri_ref, idx_ref, cnt_scr, *, C):
    aff = aff_ref[0]
    E, S = aff.shape
    n_t = S // LANES
    bits = pltpu.bitcast(aff, jnp.int32)

    def fits(cand):
        return jnp.sum((bits >= cand).astype(F32), axis=1, keepdims=True) >= C

    def search(i, t):
        hi = jnp.left_shift(jnp.int32(1), 30 - 2 * i)
        lo = jnp.left_shift(jnp.int32(1), 29 - 2 * i)
        return jnp.where(fits(t | hi | lo), t | hi | lo,
                         jnp.where(fits(t | hi), t | hi, jnp.where(fits(t | lo), t | lo, t)))

    t = lax.fori_loop(0, 15, search, jnp.zeros((E, 1), jnp.int32))
    t = jnp.where(fits(t | 1), t | 1, t)
    above = bits > t
    tied = bits == t
    n_ties = C - jnp.sum(above.astype(F32), axis=1, keepdims=True)

    def running_count(mask):
        stacked = jnp.concatenate([mask[:, k * LANES:(k + 1) * LANES] for k in range(n_t)], axis=0)
        both = jnp.dot(stacked.astype(BF16), tri_ref[...], preferred_element_type=F32)
        out, carry = [], jnp.zeros((E, LANES), F32)
        for k in range(n_t):
            out.append(both[k * E:(k + 1) * E, :LANES] + carry)
            carry = carry + both[k * E:(k + 1) * E, LANES:]
        return jnp.concatenate(out, axis=1)

    chosen = above | (tied & (running_count(tied) <= n_ties))
    cnt_scr[...] = running_count(chosen)
    lane = lax.broadcasted_iota(jnp.int32, (1, LANES), 1)
    c_blk = lax.broadcasted_iota(jnp.int32, (LANES, 1), 0).astype(F32).astype(BF16)
    one, zero = jnp.ones((), BF16), jnp.zeros((), BF16)

    def invert(e, acc):
        row = cnt_scr[pl.ds(e, 1), :]
        cols = []
        for blk in range(C // LANES):
            local = jnp.clip(row - blk * LANES, -1.0, LANES + 1.0).astype(BF16)
            part = None
            for k in range(n_t):
                hit = jnp.where(local[:, k * LANES:(k + 1) * LANES] <= c_blk, one, zero)
                part = hit if part is None else part + hit
            cols.append(jnp.sum(part.astype(F32), axis=1, keepdims=True))
        n_before = jnp.concatenate(cols, axis=0)
        return jnp.where(lane == e, n_before, acc)

    table = lax.fori_loop(0, E, invert, jnp.zeros((C, LANES), F32))
    idx_ref[0] = table.T[:E].astype(jnp.int32)


def _select(aff, C):
    B, E, S = aff.shape
    tri = (jnp.arange(LANES)[:, None] <= jnp.arange(LANES)[None, :]).astype(BF16)
    tri = jnp.concatenate([tri, jnp.ones_like(tri)], axis=1)
    return pl.pallas_call(
        functools.partial(_select_kernel, C=C),
        grid=(B,),
        in_specs=[pl.BlockSpec((1, E, S), lambda b: (b, 0, 0)),
                  pl.BlockSpec((LANES, 2 * LANES), lambda b: (0, 0))],
        out_specs=pl.BlockSpec((1, E, C), lambda b: (b, 0, 0)),
        out_shape=jax.ShapeDtypeStruct((B, E, C), jnp.int32),
        scratch_shapes=[pltpu.VMEM((E, S), F32)],
        compiler_params=pltpu.CompilerParams(dimension_semantics=("arbitrary",),
                                             vmem_limit_bytes=VMEM_LIMIT),
        name="select",
    )(aff, tri)


GATHER_GROUP = 16
SCATTER_GROUP = 8


def _moe_kernel(idx_ref, idx_prev_ref, idx_next_ref, aff_ref, aff_prev_ref, x_hbm, gf_ref,
                wg_ref, wu_ref, wd_ref, out_hbm, big, gt, yt, lhs, sem_in, sem_out, *, n_d):
    b, e = pl.program_id(0), pl.program_id(1)
    n_b, n_e = pl.num_programs(0), pl.num_programs(1)
    C, D = lhs.shape
    par = e % 2
    X, ACC = 0, 1

    def load(row):
        return pltpu.make_async_copy(x_hbm.at[row], big.at[X], sem_in)

    def store(row):
        return pltpu.make_async_copy(big.at[ACC], out_hbm.at[row], sem_out)

    def token_rows(t):
        return pl.ds(pl.multiple_of(t * n_d, n_d), n_d)

    def scatter_group(c0, idx, aff, y_par):
        updates = []
        for i in range(SCATTER_GROUP):
            t = idx[0, 0, c0 + i]
            rows = token_rows(t)
            updates.append((rows, big[ACC, rows, :] + yt[y_par, token_rows(c0 + i), :] * aff[0, 0, t]))
        for rows, val in updates:
            big[ACC, rows, :] = val

    @pl.when((b == 0) & (e == 0))
    def _():
        load(0).start()

    @pl.when(e == 0)
    def _():
        load(b).wait()

        def gather(g, carry):
            for i in range(GATHER_GROUP):
                c = g * GATHER_GROUP + i
                gt[0, token_rows(c), :] = big[X, token_rows(idx_ref[0, 0, c]), :]
            return carry

        lax.fori_loop(0, C // GATHER_GROUP, gather, 0)

    @pl.when((e == n_e - 1) & (b + 1 < n_b))
    def _():
        load(b + 1).start()

    def step(with_prev):
        xs = [gt[par, pl.ds(j, C, stride=n_d), :] for j in range(n_d)]
        rs = _inv_rms(xs, D)
        for j in range(n_d):
            cols = slice(j * LANES, (j + 1) * LANES)
            lhs[:, cols] = (gt[par, pl.ds(j, C, stride=n_d), :] * rs * gf_ref[:, cols]).astype(BF16)
        src = jnp.where(e == n_e - 1, ACC, X)
        h = lhs[...]
        pieces = n_d // 2
        row_blocks = 4
        rb = C // row_blocks
        share = C // (pieces * 2 * row_blocks)
        slot = 0
        for n in range(pieces):
            fcols = slice(2 * n * LANES, (2 * n + 2) * LANES)
            a = jnp.dot(h, wg_ref[0, 0, :, fcols], preferred_element_type=F32)
            u = jnp.dot(h, wu_ref[0, 0, :, fcols], preferred_element_type=F32)
            act = a / (1.0 + jnp.exp(-a)) * u
            for r in range(row_blocks):
                for jj in range(2):
                    base = (2 * n + jj) * C + r * rb
                    gt[par, base:base + rb, :] = act[r * rb:(r + 1) * rb, jj * LANES:(jj + 1) * LANES]
                    for c in range(slot * share, (slot + 1) * share):
                        gt[1 - par, token_rows(c), :] = big[src, token_rows(idx_next_ref[0, 0, c]), :]
                    slot += 1
        hid = jnp.concatenate([gt[par, j * C:(j + 1) * C, :].astype(BF16) for j in range(n_d)], axis=1)
        slot = 0
        for n in range(pieces):
            ncols = slice(2 * n * LANES, (2 * n + 2) * LANES)
            part = jnp.dot(hid, wd_ref[0, 0, :, ncols], preferred_element_type=F32)
            for r in range(row_blocks):
                for jj in range(2):
                    if with_prev:
                        for c0 in range(slot * share, (slot + 1) * share, SCATTER_GROUP):
                            scatter_group(c0, idx_prev_ref, aff_prev_ref, 1 - par)
                    slot += 1
                    yt[par, pl.ds(r * rb * n_d + 2 * n + jj, rb, stride=n_d), :] = (
                        part[r * rb:(r + 1) * rb, jj * LANES:(jj + 1) * LANES])

    @pl.when(e == 0)
    def _():
        step(False)

        @pl.when(b > 0)
        def _():
            store(b - 1).wait()

        big[ACC] = big[X]

    @pl.when(e > 0)
    def _():
        step(True)

    @pl.when(e == n_e - 1)
    def _():
        def scatter(g, carry):
            scatter_group(g * SCATTER_GROUP, idx_ref, aff_ref, par)
            return carry

        lax.fori_loop(0, C // SCATTER_GROUP, scatter, 0)
        store(b).start()

    @pl.when((e == n_e - 1) & (b == n_b - 1))
    def _():
        store(b).wait()


def _moe(x_tt, idx, aff, gf, wg, wu, wd, layer):
    B, rows, _ = x_tt.shape
    _, E, C = idx.shape
    S = aff.shape[2]
    D, Fd = wg.shape[2], wg.shape[3]
    n_d = D // LANES
    assert E % 2 == 0 and C % GATHER_GROUP == 0 and C % (4 * n_d * SCATTER_GROUP) == 0
    assert Fd == D
    last = B * E - 1

    def smem(n, shift):
        return pl.BlockSpec((1, 1, n), lambda b, e: (jnp.clip(b * E + e + shift, 0, last), 0, 0),
                            memory_space=pltpu.SMEM)

    weight = lambda shape: pl.BlockSpec((1, 1) + shape, lambda b, e: (layer, e, 0, 0))
    idx3, aff3 = idx.reshape(B * E, 1, C), aff.reshape(B * E, 1, S)
    return pl.pallas_call(
        functools.partial(_moe_kernel, n_d=n_d),
        grid=(B, E),
        in_specs=[smem(C, 0), smem(C, -1), smem(C, 1), smem(S, 0), smem(S, -1),
                  pl.BlockSpec(memory_space=pl.ANY),
                  pl.BlockSpec((1, D), lambda b, e: (0, 0)),
                  weight((D, Fd)), weight((D, Fd)), weight((Fd, D))],
        out_specs=pl.BlockSpec(memory_space=pl.ANY),
        out_shape=jax.ShapeDtypeStruct(x_tt.shape, F32),
        scratch_shapes=[pltpu.VMEM((2, rows, LANES), F32),
                        pltpu.VMEM((2, C * n_d, LANES), F32), pltpu.VMEM((2, C * n_d, LANES), F32),
                        pltpu.VMEM((C, D), BF16),
                        pltpu.SemaphoreType.DMA, pltpu.SemaphoreType.DMA],
        compiler_params=pltpu.CompilerParams(dimension_semantics=("arbitrary", "arbitrary"),
                                             vmem_limit_bytes=MOE_VMEM_LIMIT),
        name="moe",
    )(idx3, idx3, idx3, aff3, aff3, x_tt, gf.reshape(1, D), wg, wu, wd)


def _final_norm_kernel(x_ref, g_ref, o_ref):
    _, tm, D = o_ref.shape
    xs = _row_pieces(x_ref, tm, D // LANES, True)
    rs = _inv_rms(xs, D)
    for j, xj in enumerate(xs):
        cols = slice(j * LANES, (j + 1) * LANES)
        o_ref[0, :, cols] = xj * rs * g_ref[:, cols]


def _final_norm(x_tt, g, *, S, tm):
    B = x_tt.shape[0]
    D = g.shape[0]
    n_s = S // tm
    row = pl.BlockSpec((1, tm, D), lambda i: (i // n_s, i % n_s, 0))
    return pl.pallas_call(
        _final_norm_kernel,
        grid=(B * n_s,),
        in_specs=[pl.BlockSpec((1, tm * D // LANES, LANES), lambda i: (i // n_s, i % n_s, 0)),
                  pl.BlockSpec((1, D), lambda i: (0, 0))],
        out_specs=row,
        out_shape=jax.ShapeDtypeStruct((B, S, D), F32),
        compiler_params=pltpu.CompilerParams(dimension_semantics=("arbitrary",)),
        name="final_norm",
    )(x_tt, g.reshape(1, D))


def _rope_tables(S):
    inv = 1.0 / (ROPE_THETA ** (jnp.arange(0, HEAD_DIM, 2, dtype=F32) / HEAD_DIM))
    ang = jnp.arange(S, dtype=F32)[:, None] * inv[None, :]
    cos, sin = jnp.cos(ang), jnp.sin(ang)
    reps = LANES // HEAD_DIM
    cos_t = jnp.tile(jnp.concatenate([cos, cos], axis=-1), (1, reps))
    sin_t = jnp.tile(jnp.concatenate([-sin, sin], axis=-1), (1, reps))
    return cos_t, sin_t


def kernel(x, w_in, w_out, g_attn, g_mix_a, g_mix_b, sink, g_ffn, w_router, w_gate, w_up, w_down, g_final):
    B, S, D = x.shape
    depth = w_in.shape[0]
    tm = min(512, S)
    tm_in = min(1024, S)
    C = min(CAPACITY_FACTOR * S // N_EXPERTS, S)
    cos_t, sin_t = _rope_tables(S)
    wg, wu, wd = w_gate.astype(BF16), w_up.astype(BF16), w_down.astype(BF16)
    for l in range(depth):
        tt = l > 0
        qa, ka, va, qb, kb, vb = _inproj(x, g_attn[l], w_in[l].astype(BF16), cos_t, sin_t,
                                         S=S, tm=tm_in, token_tiled=tt)
        oa = _attn_a(qa, ka, va)
        ob = _attn_b(qb, kb, vb, sink[l])
        x, aff = _outproj(oa, ob, x, g_mix_a[l], g_mix_b[l], w_out[l].astype(BF16),
                          g_ffn[l], w_router[l], S=S, tm=tm, token_tiled=tt)
        x = _moe(x, _select(aff, C), aff, g_ffn[l], wg, wu, wd, l)
    return _final_norm(x, g_final, S=S, tm=tm)
```

```python
import functools

import jax
import jax.numpy as jnp
from jax import lax
from jax.experimental import pallas as pl
from jax.experimental.pallas import tpu as pltpu

HEAD_DIM = 64
LANES = 128
N_HEADS_A = 8
DILATIONS = ((128, 1), (512, 4), (2048, 16))
N_HEADS_B = 8
N_KV_B = 2
WINDOW_B = 128
WA = N_HEADS_A * HEAD_DIM
WB_Q = N_HEADS_B * HEAD_DIM
WB_KV = N_KV_B * HEAD_DIM
ROPE_THETA = 10000.0
N_EXPERTS = 16
CAPACITY_FACTOR = 2
RMS_EPS = 1e-6
NEG_INF = -1e30
LOG2E = 1.4426950408889634
VMEM_LIMIT = 56 * 1024 * 1024
MOE_VMEM_LIMIT = 60 * 1024 * 1024

F32 = jnp.float32
BF16 = jnp.bfloat16
_NT = (((1,), (1,)), ((), ()))


def _row_pieces(x_ref, tm, n, token_tiled, start=0):
    if token_tiled:
        return [x_ref[0, pl.ds(start * n + j, tm, stride=n), :] for j in range(n)]
    return [x_ref[0, start:start + tm, j * LANES:(j + 1) * LANES] for j in range(n)]


def _inv_rms(pieces, width):
    sq = pieces[0] * pieces[0]
    for p in pieces[1:]:
        sq = sq + p * p
    return lax.rsqrt(jnp.sum(sq, axis=-1, keepdims=True) / width + RMS_EPS)


def _lower_half_mask():
    return lax.broadcasted_iota(jnp.int32, (1, LANES), 1) < HEAD_DIM


def _inproj_kernel(x_ref, g_ref, w_ref, cos_ref, sin_ref,
                   qa_ref, ka_ref, va_ref, qb_ref, kb_ref, vb_ref, h_scr, *, token_tiled):
    tm, D = h_scr.shape
    xs = _row_pieces(x_ref, tm, D // LANES, token_tiled)
    rs = _inv_rms(xs, D)
    for j, xj in enumerate(xs):
        cols = slice(j * LANES, (j + 1) * LANES)
        h_scr[:, cols] = (xj * rs * g_ref[:, cols]).astype(BF16)
    cos = cos_ref[...]
    sin = sin_ref[...]
    lane = lax.broadcasted_iota(jnp.int32, (1, LANES), 1)
    first = (lane % HEAD_DIM) < (HEAD_DIM // 2)

    def rope(t, scale):
        partner = jnp.where(first, pltpu.roll(t, LANES - HEAD_DIM // 2, 1),
                            pltpu.roll(t, HEAD_DIM // 2, 1))
        return (t * cos + partner * sin) * scale

    q_scale = HEAD_DIM ** -0.5 * LOG2E
    n_a = WA // LANES
    n_b = WB_Q // LANES
    plan = (
        (qa_ref, True, 0 * n_a, n_a, True, q_scale),
        (ka_ref, True, 1 * n_a, n_a, True, 1.0),
        (va_ref, True, 2 * n_a, n_a, False, 1.0),
        (qb_ref, False, 3 * n_a, n_b, True, q_scale),
        (kb_ref, False, 3 * n_a + n_b, 1, True, 1.0),
        (vb_ref, False, 3 * n_a + n_b + 1, 1, False, 1.0),
    )
    h = h_scr[...]
    for ref, tile_major, start, count, do_rope, scale in plan:
        for c0 in range(0, count, 2):
            width = min(2, count - c0)
            col = (start + c0) * LANES
            t = jnp.dot(h, w_ref[:, col:col + width * LANES], preferred_element_type=F32)
            for j in range(width):
                tj = t[:, j * LANES:(j + 1) * LANES]
                val = (rope(tj, scale) if do_rope else tj).astype(BF16)
                if tile_major:
                    ref[0, c0 + j] = val
                else:
                    ref[0, :, (c0 + j) * LANES:(c0 + j + 1) * LANES] = val


def _inproj(x, g, w, cos, sin, *, S, tm, token_tiled):
    B = x.shape[0]
    D = w.shape[0]
    n_s = S // tm
    n_a = WA // LANES
    a_shape = jax.ShapeDtypeStruct((B, n_a, S, LANES), BF16)
    a_spec = pl.BlockSpec((1, n_a, tm, LANES), lambda i: (i // n_s, 0, i % n_s, 0))

    def row_spec(width):
        return pl.BlockSpec((1, tm, width), lambda i: (i // n_s, i % n_s, 0))

    tab_spec = pl.BlockSpec((tm, LANES), lambda i: (i % n_s, 0))
    x_spec = (pl.BlockSpec((1, tm * D // LANES, LANES), lambda i: (i // n_s, i % n_s, 0))
              if token_tiled else row_spec(D))
    return pl.pallas_call(
        functools.partial(_inproj_kernel, token_tiled=token_tiled),
        grid=(B * n_s,),
        in_specs=[x_spec,
                  pl.BlockSpec((1, D), lambda i: (0, 0)),
                  pl.BlockSpec(w.shape, lambda i: (0, 0)),
                  tab_spec, tab_spec],
        out_specs=[a_spec, a_spec, a_spec, row_spec(WB_Q), row_spec(WB_KV), row_spec(WB_KV)],
        out_shape=[a_shape, a_shape, a_shape,
                   jax.ShapeDtypeStruct((B, S, WB_Q), BF16),
                   jax.ShapeDtypeStruct((B, S, WB_KV), BF16),
                   jax.ShapeDtypeStruct((B, S, WB_KV), BF16)],
        scratch_shapes=[pltpu.VMEM((tm, D), BF16)],
        compiler_params=pltpu.CompilerParams(dimension_semantics=("arbitrary",),
                                             vmem_limit_bytes=VMEM_LIMIT),
        name="inproj",
    )(x, g.reshape(1, D), w, cos, sin)


def _band_bias(tq, kw, hw):
    row = jnp.arange(2 * tq)[None, :, None] % tq
    col = jnp.arange(kw)[None, None, :]
    off = (jnp.arange(3) * hw)[:, None, None]
    return jnp.where(jnp.abs(row - col + off) <= hw, 0.0, NEG_INF).astype(F32)


def _band_window(q0, hw, L, kw):
    ks = jnp.clip(q0 - hw, 0, L - kw)
    case = jnp.where(q0 - hw < 0, 0, jnp.where(q0 - hw > L - kw, 2, 1))
    return ks, case


def _band_block(q, k, v, bias, lo):
    tq = q.shape[0]
    zero = jnp.zeros_like(q)
    lhs = jnp.concatenate([jnp.where(lo, q, zero), jnp.where(lo, zero, q)], axis=0)
    s = lax.dot_general(lhs, k, _NT, preferred_element_type=F32) + bias
    m = jnp.max(s, axis=1, keepdims=True)
    e = jnp.exp2(s - m).astype(BF16)
    v_ones = jnp.concatenate([v, jnp.ones_like(v)], axis=1)
    pv = jnp.dot(e, v_ones, preferred_element_type=F32)
    acc2, l2 = pv[:, :LANES], pv[:, LANES:]
    acc = jnp.where(lo, acc2[:tq], acc2[tq:])
    m_t = jnp.where(lo, m[:tq], m[tq:])
    l_t = jnp.where(lo, l2[:tq], l2[tq:])
    return acc, m_t, l_t


def _pattern_geometry(S, tq):
    geo = []
    for window, d in DILATIONS:
        L = S // d
        hw = (window // 2) // d
        geo.append((d, L, hw, min(L, tq + 2 * hw), L // tq))
    return geo


def _attn_a_kernel(q_ref, k_ref, v_ref, bias_ref, o_ref, tmp, tmp2, qd, kd, vd, acc_s, m_s, l_s, *, S, tq):
    lo = _lower_half_mask()
    geo = _pattern_geometry(S, tq)
    for src, dst in ((q_ref, qd), (k_ref, kd), (v_ref, vd)):
        tmp[...] = src[0, 0].astype(F32)
        prev, nxt, d_prev, L_prev = tmp, tmp2, 1, S
        for p, (d, L, _, _, _) in enumerate(geo):
            if d == 1:
                continue
            keep_f32 = p + 1 < len(geo)

            def regroup(r, carry, dst=dst, p=p, L=L, q=d // d_prev, d_prev=d_prev, L_prev=L_prev,
                        prev=prev, nxt=nxt, keep_f32=keep_f32):
                rows = prev[pl.ds((r % d_prev) * L_prev + r // d_prev, L, stride=q), :]
                out = pl.ds(pl.multiple_of(r * L, L), L)
                dst[p - 1, out, :] = rows.astype(BF16)
                if keep_f32:
                    nxt[out, :] = rows
                return carry

            lax.fori_loop(0, d, regroup, 0)
            prev, nxt, d_prev, L_prev = nxt, prev, d, L

    def body(j, carry):
        for p, (d, L, hw, kw, n_qb) in enumerate(geo):
            r = j // n_qb
            q0 = (j % n_qb) * tq
            ks, case = _band_window(q0, hw, L, kw)
            if d == 1:
                rows_q = pl.ds(pl.multiple_of(q0, tq), tq)
                rows_k = pl.ds(pl.multiple_of(ks, hw), kw)
                q, k, v = q_ref[0, 0, rows_q, :], k_ref[0, 0, rows_k, :], v_ref[0, 0, rows_k, :]
            else:
                rows_q = pl.ds(r + d * q0, tq, stride=d)
                dense_q = pl.ds(pl.multiple_of(r * L + q0, tq), tq)
                dense_k = pl.ds(pl.multiple_of(r * L + ks, hw), kw)
                q, k, v = qd[p - 1, dense_q, :], kd[p - 1, dense_k, :], vd[p - 1, dense_k, :]
            acc, m_t, l_t = _band_block(q, k, v, bias_ref[case], lo)
            acc_s[p, rows_q, :] = acc
            m_s[p, rows_q, :] = m_t
            l_s[p, rows_q, :] = l_t
        return carry

    lax.fori_loop(0, S // tq, body, 0, unroll=16)

    def merge(j, carry):
        rows = pl.ds(pl.multiple_of(j * tq, tq), tq)
        ms = [m_s[p, rows, :] for p in range(len(geo))]
        m = functools.reduce(jnp.maximum, ms)
        num = den = None
        for p, mp in enumerate(ms):
            w = jnp.exp2(mp - m)
            num = acc_s[p, rows, :] * w if num is None else num + acc_s[p, rows, :] * w
            den = l_s[p, rows, :] * w if den is None else den + l_s[p, rows, :] * w
        o_ref[0, 0, rows, :] = (num / den).astype(BF16)
        return carry

    lax.fori_loop(0, S // tq, merge, 0, unroll=2)


def _attn_a(qa, ka, va, *, tq=128):
    B, n_a, S, _ = qa.shape
    geo = _pattern_geometry(S, tq)
    _, _, hw, kw, _ = geo[0]
    assert all(g[2] == hw and g[3] == kw and g[1] % tq == 0 for g in geo)
    bias = _band_bias(tq, kw, hw)
    spec = pl.BlockSpec((1, 1, S, LANES), lambda b, c: (b, c, 0, 0))
    assert geo[0][0] == 1
    seq = pltpu.VMEM((S, LANES), F32)
    dilated = pltpu.VMEM((len(geo) - 1, S, LANES), BF16)
    state = pltpu.VMEM((len(geo), S, LANES), F32)
    return pl.pallas_call(
        functools.partial(_attn_a_kernel, S=S, tq=tq),
        grid=(B, n_a),
        in_specs=[spec, spec, spec, pl.BlockSpec(bias.shape, lambda b, c: (0, 0, 0))],
        out_specs=spec,
        out_shape=jax.ShapeDtypeStruct((B, n_a, S, LANES), BF16),
        scratch_shapes=[seq, seq, dilated, dilated, dilated, state, state, state],
        compiler_params=pltpu.CompilerParams(dimension_semantics=("arbitrary", "arbitrary"),
                                             vmem_limit_bytes=VMEM_LIMIT),
        name="attn_a",
    )(qa, ka, va, bias)


def _attn_b_kernel(q_ref, k_ref, v_ref, sink_ref, bias_ref, o_ref, kk, vv, *, S, tq):
    lo = _lower_half_mask()
    hw = WINDOW_B
    kw = min(S, tq + 2 * hw)
    for src, dst in ((k_ref, kk), (v_ref, vv)):
        t = src[0].astype(F32)
        sw = pltpu.roll(t, HEAD_DIM, 1)
        dst[0] = jnp.where(lo, t, sw).astype(BF16)
        dst[1] = jnp.where(lo, sw, t).astype(BF16)
    n_c = WB_Q // LANES
    tiles_per_kv = n_c // N_KV_B

    def body(i, carry):
        q0 = pl.multiple_of(i * tq, tq)
        ks, case = _band_window(q0, hw, S, kw)
        rows_q = pl.ds(q0, tq)
        rows_k = pl.ds(pl.multiple_of(ks, tq), kw)
        bias = bias_ref[case]
        for c in range(n_c):
            cols = slice(c * LANES, (c + 1) * LANES)
            j = c // tiles_per_kv
            acc, m_t, l_t = _band_block(q_ref[0, rows_q, cols], kk[j, rows_k, :], vv[j, rows_k, :],
                                        bias, lo)
            sk = sink_ref[c:c + 1, :]
            m2 = jnp.maximum(m_t, sk)
            a = jnp.exp2(m_t - m2)
            den = l_t * a + jnp.exp2(sk - m2)
            o_ref[0, rows_q, cols] = (acc * a / den).astype(BF16)
        return carry

    lax.fori_loop(0, S // tq, body, 0, unroll=8)


def _attn_b(qb, kb, vb, sink, *, tq=128):
    B, S, _ = qb.shape
    n_c = WB_Q // LANES
    kw = min(S, tq + 2 * WINDOW_B)
    assert S % tq == 0 and tq == WINDOW_B
    bias = _band_bias(tq, kw, WINDOW_B)
    sink_tab = jnp.repeat(sink.reshape(n_c, LANES // HEAD_DIM).astype(F32) * LOG2E, HEAD_DIM, axis=1)
    kv_spec = pl.BlockSpec((1, S, WB_KV), lambda b: (b, 0, 0))
    q_spec = pl.BlockSpec((1, S, WB_Q), lambda b: (b, 0, 0))
    return pl.pallas_call(
        functools.partial(_attn_b_kernel, S=S, tq=tq),
        grid=(B,),
        in_specs=[q_spec, kv_spec, kv_spec, pl.BlockSpec((n_c, LANES), lambda b: (0, 0)),
                  pl.BlockSpec(bias.shape, lambda b: (0, 0, 0))],
        out_specs=q_spec,
        out_shape=jax.ShapeDtypeStruct((B, S, WB_Q), BF16),
        scratch_shapes=[pltpu.VMEM((N_KV_B, S, LANES), BF16)] * 2,
        compiler_params=pltpu.CompilerParams(dimension_semantics=("arbitrary",),
                                             vmem_limit_bytes=VMEM_LIMIT),
        name="attn_b",
    )(qb, kb, vb, sink_tab, bias)


def _outproj_kernel(oa_ref, ob_ref, x_ref, ga_ref, gb_ref, w_ref, gf_ref, wr_ref,
                    x_out_ref, aff_ref, mix_scr, *, token_tiled):
    tm, D = mix_scr.shape[0], w_ref.shape[1]
    n_d = D // LANES
    n_a = WA // LANES
    E = wr_ref.shape[0] // 2
    a = [oa_ref[0, c].astype(F32) for c in range(n_a)]
    rs_a = _inv_rms(a, WA)
    for c in range(n_a):
        cols = slice(c * LANES, (c + 1) * LANES)
        mix_scr[:, cols] = (a[c] * rs_a * ga_ref[:, cols]).astype(BF16)
    b = ob_ref[0].astype(F32)
    rs_b = lax.rsqrt(jnp.mean(b * b, axis=-1, keepdims=True) + RMS_EPS)
    mix_scr[:, WA:] = (b * rs_b * gb_ref[...]).astype(BF16)
    y = jnp.dot(mix_scr[...], w_ref[...], preferred_element_type=F32)
    xs = [xj + y[:, j * LANES:(j + 1) * LANES]
          for j, xj in enumerate(_row_pieces(x_ref, tm, n_d, token_tiled))]
    for j, xj in enumerate(xs):
        x_out_ref[0, pl.ds(j, tm, stride=n_d), :] = xj
    rs = _inv_rms(xs, D)
    h2 = jnp.concatenate([xj * rs * gf_ref[:, j * LANES:(j + 1) * LANES]
                          for j, xj in enumerate(xs)], axis=1)
    h_hi = h2.astype(BF16)
    h_lo = (h2 - h_hi.astype(F32)).astype(BF16)
    p = lax.dot_general(wr_ref[...], h_hi, _NT, preferred_element_type=F32)
    logits = p[:E] + p[E:] + lax.dot_general(wr_ref[:E, :], h_lo, _NT, preferred_element_type=F32)
    z = jnp.exp(logits - jnp.max(logits, axis=0, keepdims=True))
    aff_ref[0] = z / jnp.sum(z, axis=0, keepdims=True)


def _outproj(oa, ob, x, ga, gb, w, gf, w_router, *, S, tm, token_tiled):
    B = x.shape[0]
    D = w.shape[1]
    n_s = S // tm
    n_a = WA // LANES
    E = w_router.shape[1]
    wr_hi = w_router.T.astype(BF16)
    wr_lo = (w_router.T - wr_hi.astype(F32)).astype(BF16)
    wr_t = jnp.concatenate([wr_hi, wr_lo], axis=0)
    const = lambda shape: pl.BlockSpec(shape, lambda i: (0,) * len(shape))
    row = lambda width: pl.BlockSpec((1, tm, width), lambda i: (i // n_s, i % n_s, 0))
    tiled = pl.BlockSpec((1, tm * D // LANES, LANES), lambda i: (i // n_s, i % n_s, 0))
    return pl.pallas_call(
        functools.partial(_outproj_kernel, token_tiled=token_tiled),
        grid=(B * n_s,),
        in_specs=[pl.BlockSpec((1, n_a, tm, LANES), lambda i: (i // n_s, 0, i % n_s, 0)),
                  row(WB_Q), tiled if token_tiled else row(D), const((1, WA)), const((1, WB_Q)), const(w.shape),
                  const((1, D)), const(wr_t.shape)],
        out_specs=[tiled, pl.BlockSpec((1, E, tm), lambda i: (i // n_s, 0, i % n_s))],
        out_shape=[jax.ShapeDtypeStruct((B, S * D // LANES, LANES), F32),
                   jax.ShapeDtypeStruct((B, E, S), F32)],
        scratch_shapes=[pltpu.VMEM((tm, WA + WB_Q), BF16)],
        compiler_params=pltpu.CompilerParams(dimension_semantics=("arbitrary",),
                                             vmem_limit_bytes=VMEM_LIMIT),
        name="outproj",
    )(oa, ob, x, ga.reshape(1, WA), gb.reshape(1, WB_Q), w, gf.reshape(1, D), wr_t)


def _select_kernel(aff_ref, tri_ref, idx_ref, cnt_scr, *, C):
    aff = aff_ref[0]
    E, S = aff.shape
    n_t = S // LANES
    bits = pltpu.bitcast(aff, jnp.int32)

    def fits(cand):
        return jnp.sum((bits >= cand).astype(F32), axis=1, keepdims=True) >= C

    def search(i, t):
        hi = jnp.left_shift(jnp.int32(1), 30 - 2 * i)
        lo = jnp.left_shift(jnp.int32(1), 29 - 2 * i)
        return jnp.where(fits(t | hi | lo), t | hi | lo,
                         jnp.where(fits(t | hi), t | hi, jnp.where(fits(t | lo), t | lo, t)))

    t = lax.fori_loop(0, 15, search, jnp.zeros((E, 1), jnp.int32))
    t = jnp.where(fits(t | 1), t | 1, t)
    above = bits > t
    tied = bits == t
    n_ties = C - jnp.sum(above.astype(F32), axis=1, keepdims=True)

    def running_count(mask):
        stacked = jnp.concatenate([mask[:, k * LANES:(k + 1) * LANES] for k in range(n_t)], axis=0)
        both = jnp.dot(stacked.astype(BF16), tri_ref[...], preferred_element_type=F32)
        out, carry = [], jnp.zeros((E, LANES), F32)
        for k in range(n_t):
            out.append(both[k * E:(k + 1) * E, :LANES] + carry)
            carry = carry + both[k * E:(k + 1) * E, LANES:]
        return jnp.concatenate(out, axis=1)

    chosen = above | (tied & (running_count(tied) <= n_ties))
    cnt_scr[...] = running_count(chosen)
    lane = lax.broadcasted_iota(jnp.int32, (1, LANES), 1)
    c_blk = lax.broadcasted_iota(jnp.int32, (LANES, 1), 0).astype(F32).astype(BF16)
    one, zero = jnp.ones((), BF16), jnp.zeros((), BF16)

    def invert(e, acc):
        row = cnt_scr[pl.ds(e, 1), :]
        cols = []
        for blk in range(C // LANES):
            local = jnp.clip(row - blk * LANES, -1.0, LANES + 1.0).astype(BF16)
            part = None
            for k in range(n_t):
                hit = jnp.where(local[:, k * LANES:(k + 1) * LANES] <= c_blk, one, zero)
                part = hit if part is None else part + hit
            cols.append(jnp.sum(part.astype(F32), axis=1, keepdims=True))
        n_before = jnp.concatenate(cols, axis=0)
        return jnp.where(lane == e, n_before, acc)

    table = lax.fori_loop(0, E, invert, jnp.zeros((C, LANES), F32))
    idx_ref[0] = table.T[:E].astype(jnp.int32)


def _select(aff, C):
    B, E, S = aff.shape
    tri = (jnp.arange(LANES)[:, None] <= jnp.arange(LANES)[None, :]).astype(BF16)
    tri = jnp.concatenate([tri, jnp.ones_like(tri)], axis=1)
    return pl.pallas_call(
        functools.partial(_select_kernel, C=C),
        grid=(B,),
        in_specs=[pl.BlockSpec((1, E, S), lambda b: (b, 0, 0)),
                  pl.BlockSpec((LANES, 2 * LANES), lambda b: (0, 0))],
        out_specs=pl.BlockSpec((1, E, C), lambda b: (b, 0, 0)),
        out_shape=jax.ShapeDtypeStruct((B, E, C), jnp.int32),
        scratch_shapes=[pltpu.VMEM((E, S), F32)],
        compiler_params=pltpu.CompilerParams(dimension_semantics=("arbitrary",),
                                             vmem_limit_bytes=VMEM_LIMIT),
        name="select",
    )(aff, tri)


GATHER_GROUP = 16
SCATTER_GROUP = 8


def _moe_kernel(idx_ref, idx_prev_ref, idx_next_ref, aff_ref, aff_prev_ref, x_hbm, gf_ref,
                wg_ref, wu_ref, wd_ref, out_hbm, big, gt, yt, lhs, sem_in, sem_out, *, n_d):
    b, e = pl.program_id(0), pl.program_id(1)
    n_b, n_e = pl.num_programs(0), pl.num_programs(1)
    C, D = lhs.shape
    par = e % 2
    X, ACC = 0, 1

    def load(row):
        return pltpu.make_async_copy(x_hbm.at[row], big.at[X], sem_in)

    def store(row):
        return pltpu.make_async_copy(big.at[ACC], out_hbm.at[row], sem_out)

    def token_rows(t):
        return pl.ds(pl.multiple_of(t * n_d, n_d), n_d)

    def scatter_group(c0, idx, aff, y_par):
        updates = []
        for i in range(SCATTER_GROUP):
            t = idx[0, 0, c0 + i]
            rows = token_rows(t)
            updates.append((rows, big[ACC, rows, :] + yt[y_par, token_rows(c0 + i), :] * aff[0, 0, t]))
        for rows, val in updates:
            big[ACC, rows, :] = val

    @pl.when((b == 0) & (e == 0))
    def _():
        load(0).start()

    @pl.when(e == 0)
    def _():
        load(b).wait()

        def gather(g, carry):
            for i in range(GATHER_GROUP):
                c = g * GATHER_GROUP + i
                gt[0, token_rows(c), :] = big[X, token_rows(idx_ref[0, 0, c]), :]
            return carry

        lax.fori_loop(0, C // GATHER_GROUP, gather, 0)

    @pl.when((e == n_e - 1) & (b + 1 < n_b))
    def _():
        load(b + 1).start()

    def step(with_prev):
        xs = [gt[par, pl.ds(j, C, stride=n_d), :] for j in range(n_d)]
        rs = _inv_rms(xs, D)
        for j in range(n_d):
            cols = slice(j * LANES, (j + 1) * LANES)
            lhs[:, cols] = (gt[par, pl.ds(j, C, stride=n_d), :] * rs * gf_ref[:, cols]).astype(BF16)
        src = jnp.where(e == n_e - 1, ACC, X)
        h = lhs[...]
        pieces = n_d // 2
        row_blocks = 4
        rb = C // row_blocks
        share = C // (pieces * 2 * row_blocks)
        slot = 0
        for n in range(pieces):
            fcols = slice(2 * n * LANES, (2 * n + 2) * LANES)
            a = jnp.dot(h, wg_ref[0, 0, :, fcols], preferred_element_type=F32)
            u = jnp.dot(h, wu_ref[0, 0, :, fcols], preferred_element_type=F32)
            act = a / (1.0 + jnp.exp(-a)) * u
            for r in range(row_blocks):
                for jj in range(2):
                    base = (2 * n + jj) * C + r * rb
                    gt[par, base:base + rb, :] = act[r * rb:(r + 1) * rb, jj * LANES:(jj + 1) * LANES]
                    for c in range(slot * share, (slot + 1) * share):
                        gt[1 - par, token_rows(c), :] = big[src, token_rows(idx_next_ref[0, 0, c]), :]
                    slot += 1
        hid = jnp.concatenate([gt[par, j * C:(j + 1) * C, :].astype(BF16) for j in range(n_d)], axis=1)
        slot = 0
        for n in range(pieces):
            ncols = slice(2 * n * LANES, (2 * n + 2) * LANES)
            part = jnp.dot(hid, wd_ref[0, 0, :, ncols], preferred_element_type=F32)
            for r in range(row_blocks):
                for jj in range(2):
                    if with_prev:
                        for c0 in range(slot * share, (slot + 1) * share, SCATTER_GROUP):
                            scatter_group(c0, idx_prev_ref, aff_prev_ref, 1 - par)
                    slot += 1
                    yt[par, pl.ds(r * rb * n_d + 2 * n + jj, rb, stride=n_d), :] = (
                        part[r * rb:(r + 1) * rb, jj * LANES:(jj + 1) * LANES])

    @pl.when(e == 0)
    def _():
        step(False)

        @pl.when(b > 0)
        def _():
            store(b - 1).wait()

        big[ACC] = big[X]

    @pl.when(e > 0)
    def _():
        step(True)

    @pl.when(e == n_e - 1)
    def _():
        def scatter(g, carry):
            scatter_group(g * SCATTER_GROUP, idx_ref, aff_ref, par)
            return carry

        lax.fori_loop(0, C // SCATTER_GROUP, scatter, 0)
        store(b).start()

    @pl.when((e == n_e - 1) & (b == n_b - 1))
    def _():
        store(b).wait()


def _moe(x_tt, idx, aff, gf, wg, wu, wd, layer):
    B, rows, _ = x_tt.shape
    _, E, C = idx.shape
    S = aff.shape[2]
    D, Fd = wg.shape[2], wg.shape[3]
    n_d = D // LANES
    assert E % 2 == 0 and C % GATHER_GROUP == 0 and C % (4 * n_d * SCATTER_GROUP) == 0
    assert Fd == D
    last = B * E - 1

    def smem(n, shift):
        return pl.BlockSpec((1, 1, n), lambda b, e: (jnp.clip(b * E + e + shift, 0, last), 0, 0),
                            memory_space=pltpu.SMEM)

    weight = lambda shape: pl.BlockSpec((1, 1) + shape, lambda b, e: (layer, e, 0, 0))
    idx3, aff3 = idx.reshape(B * E, 1, C), aff.reshape(B * E, 1, S)
    return pl.pallas_call(
        functools.partial(_moe_kernel, n_d=n_d),
        grid=(B, E),
        in_specs=[smem(C, 0), smem(C, -1), smem(C, 1), smem(S, 0), smem(S, -1),
                  pl.BlockSpec(memory_space=pl.ANY),
                  pl.BlockSpec((1, D), lambda b, e: (0, 0)),
                  weight((D, Fd)), weight((D, Fd)), weight((Fd, D))],
        out_specs=pl.BlockSpec(memory_space=pl.ANY),
        out_shape=jax.ShapeDtypeStruct(x_tt.shape, F32),
        scratch_shapes=[pltpu.VMEM((2, rows, LANES), F32),
                        pltpu.VMEM((2, C * n_d, LANES), F32), pltpu.VMEM((2, C * n_d, LANES), F32),
                        pltpu.VMEM((C, D), BF16),
                        pltpu.SemaphoreType.DMA, pltpu.SemaphoreType.DMA],
        compiler_params=pltpu.CompilerParams(dimension_semantics=("arbitrary", "arbitrary"),
                                             vmem_limit_bytes=MOE_VMEM_LIMIT),
        name="moe",
    )(idx3, idx3, idx3, aff3, aff3, x_tt, gf.reshape(1, D), wg, wu, wd)


def _final_norm_kernel(x_ref, g_ref, o_ref):
    _, tm, D = o_ref.shape
    xs = _row_pieces(x_ref, tm, D // LANES, True)
    rs = _inv_rms(xs, D)
    for j, xj in enumerate(xs):
        cols = slice(j * LANES, (j + 1) * LANES)
        o_ref[0, :, cols] = xj * rs * g_ref[:, cols]


def _final_norm(x_tt, g, *, S, tm):
    B = x_tt.shape[0]
    D = g.shape[0]
    n_s = S // tm
    row = pl.BlockSpec((1, tm, D), lambda i: (i // n_s, i % n_s, 0))
    return pl.pallas_call(
        _final_norm_kernel,
        grid=(B * n_s,),
        in_specs=[pl.BlockSpec((1, tm * D // LANES, LANES), lambda i: (i // n_s, i % n_s, 0)),
                  pl.BlockSpec((1, D), lambda i: (0, 0))],
        out_specs=row,
        out_shape=jax.ShapeDtypeStruct((B, S, D), F32),
        compiler_params=pltpu.CompilerParams(dimension_semantics=("arbitrary",)),
        name="final_norm",
    )(x_tt, g.reshape(1, D))


def _rope_tables(S):
    inv = 1.0 / (ROPE_THETA ** (jnp.arange(0, HEAD_DIM, 2, dtype=F32) / HEAD_DIM))
    ang = jnp.arange(S, dtype=F32)[:, None] * inv[None, :]
    cos, sin = jnp.cos(ang), jnp.sin(ang)
    reps = LANES // HEAD_DIM
    cos_t = jnp.tile(jnp.concatenate([cos, cos], axis=-1), (1, reps))
    sin_t = jnp.tile(jnp.concatenate([-sin, sin], axis=-1), (1, reps))
    return cos_t, sin_t


def kernel(x, w_in, w_out, g_attn, g_mix_a, g_mix_b, sink, g_ffn, w_router, w_gate, w_up, w_down, g_final):
    B, S, D = x.shape
    depth = w_in.shape[0]
    tm = min(512, S)
    tm_in = min(1024, S)
    C = min(CAPACITY_FACTOR * S // N_EXPERTS, S)
    cos_t, sin_t = _rope_tables(S)
    wg, wu, wd = w_gate.astype(BF16), w_up.astype(BF16), w_down.astype(BF16)
    for l in range(depth):
        tt = l > 0
        qa, ka, va, qb, kb, vb = _inproj(x, g_attn[l], w_in[l].astype(BF16), cos_t, sin_t,
                                         S=S, tm=tm_in, token_tiled=tt)
        oa = _attn_a(qa, ka, va)
        ob = _attn_b(qb, kb, vb, sink[l])
        x, aff = _outproj(oa, ob, x, g_mix_a[l], g_mix_b[l], w_out[l].astype(BF16),
                          g_ffn[l], w_router[l], S=S, tm=tm, token_tiled=tt)
        x = _moe(x, _select(aff, C), aff, g_ffn[l], wg, wu, wd, l)
    return _final_norm(x, g_final, S=S, tm=tm)
```

```python
import functools

import jax
import jax.numpy as jnp
from jax import lax
from jax.experimental import pallas as pl
from jax.experimental.pallas import tpu as pltpu

HEAD_DIM = 64
LANES = 128
N_HEADS_A = 8
DILATIONS = ((128, 1), (512, 4), (2048, 16))
N_HEADS_B = 8
N_KV_B = 2
WINDOW_B = 128
WA = N_HEADS_A * HEAD_DIM
WB_Q = N_HEADS_B * HEAD_DIM
WB_KV = N_KV_B * HEAD_DIM
ROPE_THETA = 10000.0
N_EXPERTS = 16
CAPACITY_FACTOR = 2
RMS_EPS = 1e-6
NEG_INF = -1e30
LOG2E = 1.4426950408889634
VMEM_LIMIT = 56 * 1024 * 1024
MOE_VMEM_LIMIT = 60 * 1024 * 1024

F32 = jnp.float32
BF16 = jnp.bfloat16
_NT = (((1,), (1,)), ((), ()))


def _row_pieces(x_ref, tm, n, token_tiled, start=0):
    if token_tiled:
        return [x_ref[0, pl.ds(start * n + j, tm, stride=n), :] for j in range(n)]
    return [x_ref[0, start:start + tm, j * LANES:(j + 1) * LANES] for j in range(n)]


def _inv_rms(pieces, width):
    sq = pieces[0] * pieces[0]
    for p in pieces[1:]:
        sq = sq + p * p
    return lax.rsqrt(jnp.sum(sq, axis=-1, keepdims=True) / width + RMS_EPS)


def _lower_half_mask():
    return lax.broadcasted_iota(jnp.int32, (1, LANES), 1) < HEAD_DIM


def _inproj_kernel(x_ref, g_ref, w_ref, cos_ref, sin_ref,
                   qa_ref, ka_ref, va_ref, qb_ref, kb_ref, vb_ref, h_scr, *, token_tiled):
    tm, D = h_scr.shape
    xs = _row_pieces(x_ref, tm, D // LANES, token_tiled)
    rs = _inv_rms(xs, D)
    for j, xj in enumerate(xs):
        cols = slice(j * LANES, (j + 1) * LANES)
        h_scr[:, cols] = (xj * rs * g_ref[:, cols]).astype(BF16)
    cos = cos_ref[...]
    sin = sin_ref[...]
    lane = lax.broadcasted_iota(jnp.int32, (1, LANES), 1)
    first = (lane % HEAD_DIM) < (HEAD_DIM // 2)

    def rope(t, scale):
        partner = jnp.where(first, pltpu.roll(t, LANES - HEAD_DIM // 2, 1),
                            pltpu.roll(t, HEAD_DIM // 2, 1))
        return (t * cos + partner * sin) * scale

    q_scale = HEAD_DIM ** -0.5 * LOG2E
    n_a = WA // LANES
    n_b = WB_Q // LANES
    plan = (
        (qa_ref, True, 0 * n_a, n_a, True, q_scale),
        (ka_ref, True, 1 * n_a, n_a, True, 1.0),
        (va_ref, True, 2 * n_a, n_a, False, 1.0),
        (qb_ref, False, 3 * n_a, n_b, True, q_scale),
        (kb_ref, False, 3 * n_a + n_b, 1, True, 1.0),
        (vb_ref, False, 3 * n_a + n_b + 1, 1, False, 1.0),
    )
    h = h_scr[...]
    for ref, tile_major, start, count, do_rope, scale in plan:
        for c0 in range(0, count, 2):
            width = min(2, count - c0)
            col = (start + c0) * LANES
            t = jnp.dot(h, w_ref[:, col:col + width * LANES], preferred_element_type=F32)
            for j in range(width):
                tj = t[:, j * LANES:(j + 1) * LANES]
                val = (rope(tj, scale) if do_rope else tj).astype(BF16)
                if tile_major:
                    ref[0, c0 + j] = val
                else:
                    ref[0, :, (c0 + j) * LANES:(c0 + j + 1) * LANES] = val


def _inproj(x, g, w, cos, sin, *, S, tm, token_tiled):
    B = x.shape[0]
    D = w.shape[0]
    n_s = S // tm
    n_a = WA // LANES
    a_shape = jax.ShapeDtypeStruct((B, n_a, S, LANES), BF16)
    a_spec = pl.BlockSpec((1, n_a, tm, LANES), lambda i: (i // n_s, 0, i % n_s, 0))

    def row_spec(width):
        return pl.BlockSpec((1, tm, width), lambda i: (i // n_s, i % n_s, 0))

    tab_spec = pl.BlockSpec((tm, LANES), lambda i: (i % n_s, 0))
    x_spec = (pl.BlockSpec((1, tm * D // LANES, LANES), lambda i: (i // n_s, i % n_s, 0))
              if token_tiled else row_spec(D))
    return pl.pallas_call(
        functools.partial(_inproj_kernel, token_tiled=token_tiled),
        grid=(B * n_s,),
        in_specs=[x_spec,
                  pl.BlockSpec((1, D), lambda i: (0, 0)),
                  pl.BlockSpec(w.shape, lambda i: (0, 0)),
                  tab_spec, tab_spec],
        out_specs=[a_spec, a_spec, a_spec, row_spec(WB_Q), row_spec(WB_KV), row_spec(WB_KV)],
        out_shape=[a_shape, a_shape, a_shape,
                   jax.ShapeDtypeStruct((B, S, WB_Q), BF16),
                   jax.ShapeDtypeStruct((B, S, WB_KV), BF16),
                   jax.ShapeDtypeStruct((B, S, WB_KV), BF16)],
        scratch_shapes=[pltpu.VMEM((tm, D), BF16)],
        compiler_params=pltpu.CompilerParams(dimension_semantics=("arbitrary",),
                                             vmem_limit_bytes=VMEM_LIMIT),
        name="inproj",
    )(x, g.reshape(1, D), w, cos, sin)


def _band_bias(tq, kw, hw):
    row = jnp.arange(2 * tq)[None, :, None] % tq
    col = jnp.arange(kw)[None, None, :]
    off = (jnp.arange(3) * hw)[:, None, None]
    return jnp.where(jnp.abs(row - col + off) <= hw, 0.0, NEG_INF).astype(F32)


def _band_window(q0, hw, L, kw):
    ks = jnp.clip(q0 - hw, 0, L - kw)
    case = jnp.where(q0 - hw < 0, 0, jnp.where(q0 - hw > L - kw, 2, 1))
    return ks, case


def _band_block(q, k, v, bias, lo):
    tq = q.shape[0]
    zero = jnp.zeros_like(q)
    lhs = jnp.concatenate([jnp.where(lo, q, zero), jnp.where(lo, zero, q)], axis=0)
    s = lax.dot_general(lhs, k, _NT, preferred_element_type=F32) + bias
    m = jnp.max(s, axis=1, keepdims=True)
    e = jnp.exp2(s - m).astype(BF16)
    v_ones = jnp.concatenate([v, jnp.ones_like(v)], axis=1)
    pv = jnp.dot(e, v_ones, preferred_element_type=F32)
    acc2, l2 = pv[:, :LANES], pv[:, LANES:]
    acc = jnp.where(lo, acc2[:tq], acc2[tq:])
    m_t = jnp.where(lo, m[:tq], m[tq:])
    l_t = jnp.where(lo, l2[:tq], l2[tq:])
    return acc, m_t, l_t


def _pattern_geometry(S, tq):
    geo = []
    for window, d in DILATIONS:
        L = S // d
        hw = (window // 2) // d
        geo.append((d, L, hw, min(L, tq + 2 * hw), L // tq))
    return geo


def _attn_a_kernel(q_ref, k_ref, v_ref, bias_ref, o_ref, tmp, tmp2, qd, kd, vd, acc_s, m_s, l_s, *, S, tq):
    lo = _lower_half_mask()
    geo = _pattern_geometry(S, tq)
    for src, dst in ((q_ref, qd), (k_ref, kd), (v_ref, vd)):
        tmp[...] = src[0, 0].astype(F32)
        prev, nxt, d_prev, L_prev = tmp, tmp2, 1, S
        for p, (d, L, _, _, _) in enumerate(geo):
            if d == 1:
                continue
            keep_f32 = p + 1 < len(geo)

            def regroup(r, carry, dst=dst, p=p, L=L, q=d // d_prev, d_prev=d_prev, L_prev=L_prev,
                        prev=prev, nxt=nxt, keep_f32=keep_f32):
                rows = prev[pl.ds((r % d_prev) * L_prev + r // d_prev, L, stride=q), :]
                out = pl.ds(pl.multiple_of(r * L, L), L)
                dst[p - 1, out, :] = rows.astype(BF16)
                if keep_f32:
                    nxt[out, :] = rows
                return carry

            lax.fori_loop(0, d, regroup, 0)
            prev, nxt, d_prev, L_prev = nxt, prev, d, L

    def body(j, carry):
        for p, (d, L, hw, kw, n_qb) in enumerate(geo):
            r = j // n_qb
            q0 = (j % n_qb) * tq
            ks, case = _band_window(q0, hw, L, kw)
            if d == 1:
                rows_q = pl.ds(pl.multiple_of(q0, tq), tq)
                rows_k = pl.ds(pl.multiple_of(ks, hw), kw)
                q, k, v = q_ref[0, 0, rows_q, :], k_ref[0, 0, rows_k, :], v_ref[0, 0, rows_k, :]
            else:
                rows_q = pl.ds(r + d * q0, tq, stride=d)
                dense_q = pl.ds(pl.multiple_of(r * L + q0, tq), tq)
                dense_k = pl.ds(pl.multiple_of(r * L + ks, hw), kw)
                q, k, v = qd[p - 1, dense_q, :], kd[p - 1, dense_k, :], vd[p - 1, dense_k, :]
            acc, m_t, l_t = _band_block(q, k, v, bias_ref[case], lo)
            acc_s[p, rows_q, :] = acc
            m_s[p, rows_q, :] = m_t
            l_s[p, rows_q, :] = l_t
        return carry

    lax.fori_loop(0, S // tq, body, 0, unroll=32)

    def merge(j, carry):
        rows = pl.ds(pl.multiple_of(j * tq, tq), tq)
        ms = [m_s[p, rows, :] for p in range(len(geo))]
        m = functools.reduce(jnp.maximum, ms)
        num = den = None
        for p, mp in enumerate(ms):
            w = jnp.exp2(mp - m)
            num = acc_s[p, rows, :] * w if num is None else num + acc_s[p, rows, :] * w
            den = l_s[p, rows, :] * w if den is None else den + l_s[p, rows, :] * w
        o_ref[0, 0, rows, :] = (num / den).astype(BF16)
        return carry

    lax.fori_loop(0, S // tq, merge, 0, unroll=2)


def _attn_a(qa, ka, va, *, tq=128):
    B, n_a, S, _ = qa.shape
    geo = _pattern_geometry(S, tq)
    _, _, hw, kw, _ = geo[0]
    assert all(g[2] == hw and g[3] == kw and g[1] % tq == 0 for g in geo)
    bias = _band_bias(tq, kw, hw)
    spec = pl.BlockSpec((1, 1, S, LANES), lambda b, c: (b, c, 0, 0))
    assert geo[0][0] == 1
    seq = pltpu.VMEM((S, LANES), F32)
    dilated = pltpu.VMEM((len(geo) - 1, S, LANES), BF16)
    state = pltpu.VMEM((len(geo), S, LANES), F32)
    return pl.pallas_call(
        functools.partial(_attn_a_kernel, S=S, tq=tq),
        grid=(B, n_a),
        in_specs=[spec, spec, spec, pl.BlockSpec(bias.shape, lambda b, c: (0, 0, 0))],
        out_specs=spec,
        out_shape=jax.ShapeDtypeStruct((B, n_a, S, LANES), BF16),
        scratch_shapes=[seq, seq, dilated, dilated, dilated, state, state, state],
        compiler_params=pltpu.CompilerParams(dimension_semantics=("arbitrary", "arbitrary"),
                                             vmem_limit_bytes=VMEM_LIMIT),
        name="attn_a",
    )(qa, ka, va, bias)


def _attn_b_kernel(q_ref, k_ref, v_ref, sink_ref, bias_ref, o_ref, kk, vv, *, S, tq):
    lo = _lower_half_mask()
    hw = WINDOW_B
    kw = min(S, tq + 2 * hw)
    for src, dst in ((k_ref, kk), (v_ref, vv)):
        t = src[0].astype(F32)
        sw = pltpu.roll(t, HEAD_DIM, 1)
        dst[0] = jnp.where(lo, t, sw).astype(BF16)
        dst[1] = jnp.where(lo, sw, t).astype(BF16)
    n_c = WB_Q // LANES
    tiles_per_kv = n_c // N_KV_B

    def body(i, carry):
        q0 = pl.multiple_of(i * tq, tq)
        ks, case = _band_window(q0, hw, S, kw)
        rows_q = pl.ds(q0, tq)
        rows_k = pl.ds(pl.multiple_of(ks, tq), kw)
        bias = bias_ref[case]
        for c in range(n_c):
            cols = slice(c * LANES, (c + 1) * LANES)
            j = c // tiles_per_kv
            acc, m_t, l_t = _band_block(q_ref[0, rows_q, cols], kk[j, rows_k, :], vv[j, rows_k, :],
                                        bias, lo)
            sk = sink_ref[c:c + 1, :]
            m2 = jnp.maximum(m_t, sk)
            a = jnp.exp2(m_t - m2)
            den = l_t * a + jnp.exp2(sk - m2)
            o_ref[0, rows_q, cols] = (acc * a / den).astype(BF16)
        return carry

    lax.fori_loop(0, S // tq, body, 0, unroll=16)


def _attn_b(qb, kb, vb, sink, *, tq=128):
    B, S, _ = qb.shape
    n_c = WB_Q // LANES
    kw = min(S, tq + 2 * WINDOW_B)
    assert S % tq == 0 and tq == WINDOW_B
    bias = _band_bias(tq, kw, WINDOW_B)
    sink_tab = jnp.repeat(sink.reshape(n_c, LANES // HEAD_DIM).astype(F32) * LOG2E, HEAD_DIM, axis=1)
    kv_spec = pl.BlockSpec((1, S, WB_KV), lambda b: (b, 0, 0))
    q_spec = pl.BlockSpec((1, S, WB_Q), lambda b: (b, 0, 0))
    return pl.pallas_call(
        functools.partial(_attn_b_kernel, S=S, tq=tq),
        grid=(B,),
        in_specs=[q_spec, kv_spec, kv_spec, pl.BlockSpec((n_c, LANES), lambda b: (0, 0)),
                  pl.BlockSpec(bias.shape, lambda b: (0, 0, 0))],
        out_specs=q_spec,
        out_shape=jax.ShapeDtypeStruct((B, S, WB_Q), BF16),
        scratch_shapes=[pltpu.VMEM((N_KV_B, S, LANES), BF16)] * 2,
        compiler_params=pltpu.CompilerParams(dimension_semantics=("arbitrary",),
                                             vmem_limit_bytes=VMEM_LIMIT),
        name="attn_b",
    )(qb, kb, vb, sink_tab, bias)


def _outproj_kernel(oa_ref, ob_ref, x_ref, ga_ref, gb_ref, w_ref, gf_ref, wr_ref,
                    x_out_ref, aff_ref, mix_scr, *, token_tiled):
    tm, D = mix_scr.shape[0], w_ref.shape[1]
    n_d = D // LANES
    n_a = WA // LANES
    E = wr_ref.shape[0] // 2
    a = [oa_ref[0, c].astype(F32) for c in range(n_a)]
    rs_a = _inv_rms(a, WA)
    for c in range(n_a):
        cols = slice(c * LANES, (c + 1) * LANES)
        mix_scr[:, cols] = (a[c] * rs_a * ga_ref[:, cols]).astype(BF16)
    b = ob_ref[0].astype(F32)
    rs_b = lax.rsqrt(jnp.mean(b * b, axis=-1, keepdims=True) + RMS_EPS)
    mix_scr[:, WA:] = (b * rs_b * gb_ref[...]).astype(BF16)
    y = jnp.dot(mix_scr[...], w_ref[...], preferred_element_type=F32)
    xs = [xj + y[:, j * LANES:(j + 1) * LANES]
          for j, xj in enumerate(_row_pieces(x_ref, tm, n_d, token_tiled))]
    for j, xj in enumerate(xs):
        x_out_ref[0, pl.ds(j, tm, stride=n_d), :] = xj
    rs = _inv_rms(xs, D)
    h2 = jnp.concatenate([xj * rs * gf_ref[:, j * LANES:(j + 1) * LANES]
                          for j, xj in enumerate(xs)], axis=1)
    h_hi = h2.astype(BF16)
    h_lo = (h2 - h_hi.astype(F32)).astype(BF16)
    p = lax.dot_general(wr_ref[...], h_hi, _NT, preferred_element_type=F32)
    logits = p[:E] + p[E:] + lax.dot_general(wr_ref[:E, :], h_lo, _NT, preferred_element_type=F32)
    z = jnp.exp(logits - jnp.max(logits, axis=0, keepdims=True))
    aff_ref[0] = z / jnp.sum(z, axis=0, keepdims=True)


def _outproj(oa, ob, x, ga, gb, w, gf, w_router, *, S, tm, token_tiled):
    B = x.shape[0]
    D = w.shape[1]
    n_s = S // tm
    n_a = WA // LANES
    E = w_router.shape[1]
    wr_hi = w_router.T.astype(BF16)
    wr_lo = (w_router.T - wr_hi.astype(F32)).astype(BF16)
    wr_t = jnp.concatenate([wr_hi, wr_lo], axis=0)
    const = lambda shape: pl.BlockSpec(shape, lambda i: (0,) * len(shape))
    row = lambda width: pl.BlockSpec((1, tm, width), lambda i: (i // n_s, i % n_s, 0))
    tiled = pl.BlockSpec((1, tm * D // LANES, LANES), lambda i: (i // n_s, i % n_s, 0))
    return pl.pallas_call(
        functools.partial(_outproj_kernel, token_tiled=token_tiled),
        grid=(B * n_s,),
        in_specs=[pl.BlockSpec((1, n_a, tm, LANES), lambda i: (i // n_s, 0, i % n_s, 0)),
                  row(WB_Q), tiled if token_tiled else row(D), const((1, WA)), const((1, WB_Q)), const(w.shape),
                  const((1, D)), const(wr_t.shape)],
        out_specs=[tiled, pl.BlockSpec((1, E, tm), lambda i: (i // n_s, 0, i % n_s))],
        out_shape=[jax.ShapeDtypeStruct((B, S * D // LANES, LANES), F32),
                   jax.ShapeDtypeStruct((B, E, S), F32)],
        scratch_shapes=[pltpu.VMEM((tm, WA + WB_Q), BF16)],
        compiler_params=pltpu.CompilerParams(dimension_semantics=("arbitrary",),
                                             vmem_limit_bytes=VMEM_LIMIT),
        name="outproj",
    )(oa, ob, x, ga.reshape(1, WA), gb.reshape(1, WB_Q), w, gf.reshape(1, D), wr_t)


def _select_kernel(aff_ref, tri_ref, idx_ref, cnt_scr, *, C):
    aff = aff_ref[0]
    E, S = aff.shape
    n_t = S // LANES
    bits = pltpu.bitcast(aff, jnp.int32)

    def fits(cand):
        return jnp.sum((bits >= cand).astype(F32), axis=1, keepdims=True) >= C

    def search(i, t):
        hi = jnp.left_shift(jnp.int32(1), 30 - 2 * i)
        lo = jnp.left_shift(jnp.int32(1), 29 - 2 * i)
        return jnp.where(fits(t | hi | lo), t | hi | lo,
                         jnp.where(fits(t | hi), t | hi, jnp.where(fits(t | lo), t | lo, t)))

    t = lax.fori_loop(0, 15, search, jnp.zeros((E, 1), jnp.int32))
    t = jnp.where(fits(t | 1), t | 1, t)
    above = bits > t
    tied = bits == t
    n_ties = C - jnp.sum(above.astype(F32), axis=1, keepdims=True)

    def running_count(mask):
        stacked = jnp.concatenate([mask[:, k * LANES:(k + 1) * LANES] for k in range(n_t)], axis=0)
        both = jnp.dot(stacked.astype(BF16), tri_ref[...], preferred_element_type=F32)
        out, carry = [], jnp.zeros((E, LANES), F32)
        for k in range(n_t):
            out.append(both[k * E:(k + 1) * E, :LANES] + carry)
            carry = carry + both[k * E:(k + 1) * E, LANES:]
        return jnp.concatenate(out, axis=1)

    chosen = above | (tied & (running_count(tied) <= n_ties))
    cnt_scr[...] = running_count(chosen)
    lane = lax.broadcasted_iota(jnp.int32, (1, LANES), 1)
    c_blk = lax.broadcasted_iota(jnp.int32, (LANES, 1), 0).astype(F32).astype(BF16)
    one, zero = jnp.ones((), BF16), jnp.zeros((), BF16)

    def invert(e, acc):
        row = cnt_scr[pl.ds(e, 1), :]
        cols = []
        for blk in range(C // LANES):
            local = jnp.clip(row - blk * LANES, -1.0, LANES + 1.0).astype(BF16)
            part = None
            for k in range(n_t):
                hit = jnp.where(local[:, k * LANES:(k + 1) * LANES] <= c_blk, one, zero)
                part = hit if part is None else part + hit
            cols.append(jnp.sum(part.astype(F32), axis=1, keepdims=True))
        n_before = jnp.concatenate(cols, axis=0)
        return jnp.where(lane == e, n_before, acc)

    table = lax.fori_loop(0, E, invert, jnp.zeros((C, LANES), F32))
    idx_ref[0] = table.T[:E].astype(jnp.int32)


def _select(aff, C):
    B, E, S = aff.shape
    tri = (jnp.arange(LANES)[:, None] <= jnp.arange(LANES)[None, :]).astype(BF16)
    tri = jnp.concatenate([tri, jnp.ones_like(tri)], axis=1)
    return pl.pallas_call(
        functools.partial(_select_kernel, C=C),
        grid=(B,),
        in_specs=[pl.BlockSpec((1, E, S), lambda b: (b, 0, 0)),
                  pl.BlockSpec((LANES, 2 * LANES), lambda b: (0, 0))],
        out_specs=pl.BlockSpec((1, E, C), lambda b: (b, 0, 0)),
        out_shape=jax.ShapeDtypeStruct((B, E, C), jnp.int32),
        scratch_shapes=[pltpu.VMEM((E, S), F32)],
        compiler_params=pltpu.CompilerParams(dimension_semantics=("arbitrary",),
                                             vmem_limit_bytes=VMEM_LIMIT),
        name="select",
    )(aff, tri)


GATHER_GROUP = 16
SCATTER_GROUP = 8


def _moe_kernel(idx_ref, idx_prev_ref, idx_next_ref, aff_ref, aff_prev_ref, x_hbm, gf_ref,
                wg_ref, wu_ref, wd_ref, out_hbm, big, gt, yt, lhs, sem_in, sem_out, *, n_d):
    b, e = pl.program_id(0), pl.program_id(1)
    n_b, n_e = pl.num_programs(0), pl.num_programs(1)
    C, D = lhs.shape
    par = e % 2
    X, ACC = 0, 1

    def load(row):
        return pltpu.make_async_copy(x_hbm.at[row], big.at[X], sem_in)

    def store(row):
        return pltpu.make_async_copy(big.at[ACC], out_hbm.at[row], sem_out)

    def token_rows(t):
        return pl.ds(pl.multiple_of(t * n_d, n_d), n_d)

    def scatter_group(c0, idx, aff, y_par):
        updates = []
        for i in range(SCATTER_GROUP):
            t = idx[0, 0, c0 + i]
            rows = token_rows(t)
            updates.append((rows, big[ACC, rows, :] + yt[y_par, token_rows(c0 + i), :] * aff[0, 0, t]))
        for rows, val in updates:
            big[ACC, rows, :] = val

    @pl.when((b == 0) & (e == 0))
    def _():
        load(0).start()

    @pl.when(e == 0)
    def _():
        load(b).wait()

        def gather(g, carry):
            for i in range(GATHER_GROUP):
                c = g * GATHER_GROUP + i
                gt[0, token_rows(c), :] = big[X, token_rows(idx_ref[0, 0, c]), :]
            return carry

        lax.fori_loop(0, C // GATHER_GROUP, gather, 0)

    @pl.when((e == n_e - 1) & (b + 1 < n_b))
    def _():
        load(b + 1).start()

    def step(with_prev):
        xs = [gt[par, pl.ds(j, C, stride=n_d), :] for j in range(n_d)]
        rs = _inv_rms(xs, D)
        for j in range(n_d):
            cols = slice(j * LANES, (j + 1) * LANES)
            lhs[:, cols] = (gt[par, pl.ds(j, C, stride=n_d), :] * rs * gf_ref[:, cols]).astype(BF16)
        src = jnp.where(e == n_e - 1, ACC, X)
        h = lhs[...]
        pieces = n_d // 2
        row_blocks = 4
        rb = C // row_blocks
        share = C // (pieces * 2 * row_blocks)
        slot = 0
        for n in range(pieces):
            fcols = slice(2 * n * LANES, (2 * n + 2) * LANES)
            a = jnp.dot(h, wg_ref[0, 0, :, fcols], preferred_element_type=F32)
            u = jnp.dot(h, wu_ref[0, 0, :, fcols], preferred_element_type=F32)
            act = a / (1.0 + jnp.exp(-a)) * u
            for r in range(row_blocks):
                for jj in range(2):
                    base = (2 * n + jj) * C + r * rb
                    gt[par, base:base + rb, :] = act[r * rb:(r + 1) * rb, jj * LANES:(jj + 1) * LANES]
                    for c in range(slot * share, (slot + 1) * share):
                        gt[1 - par, token_rows(c), :] = big[src, token_rows(idx_next_ref[0, 0, c]), :]
                    slot += 1
        hid = jnp.concatenate([gt[par, j * C:(j + 1) * C, :].astype(BF16) for j in range(n_d)], axis=1)
        slot = 0
        for n in range(pieces):
            ncols = slice(2 * n * LANES, (2 * n + 2) * LANES)
            part = jnp.dot(hid, wd_ref[0, 0, :, ncols], preferred_element_type=F32)
            for r in range(row_blocks):
                for jj in range(2):
                    if with_prev:
                        for c0 in range(slot * share, (slot + 1) * share, SCATTER_GROUP):
                            scatter_group(c0, idx_prev_ref, aff_prev_ref, 1 - par)
                    slot += 1
                    yt[par, pl.ds(r * rb * n_d + 2 * n + jj, rb, stride=n_d), :] = (
                        part[r * rb:(r + 1) * rb, jj * LANES:(jj + 1) * LANES])

    @pl.when(e == 0)
    def _():
        step(False)

        @pl.when(b > 0)
        def _():
            store(b - 1).wait()

        big[ACC] = big[X]

    @pl.when(e > 0)
    def _():
        step(True)

    @pl.when(e == n_e - 1)
    def _():
        def scatter(g, carry):
            scatter_group(g * SCATTER_GROUP, idx_ref, aff_ref, par)
            return carry

        lax.fori_loop(0, C // SCATTER_GROUP, scatter, 0)
        store(b).start()

    @pl.when((e == n_e - 1) & (b == n_b - 1))
    def _():
        store(b).wait()


def _moe(x_tt, idx, aff, gf, wg, wu, wd, layer):
    B, rows, _ = x_tt.shape
    _, E, C = idx.shape
    S = aff.shape[2]
    D, Fd = wg.shape[2], wg.shape[3]
    n_d = D // LANES
    assert E % 2 == 0 and C % GATHER_GROUP == 0 and C % (4 * n_d * SCATTER_GROUP) == 0
    assert Fd == D
    last = B * E - 1

    def smem(n, shift):
        return pl.BlockSpec((1, 1, n), lambda b, e: (jnp.clip(b * E + e + shift, 0, last), 0, 0),
                            memory_space=pltpu.SMEM)

    weight = lambda shape: pl.BlockSpec((1, 1) + shape, lambda b, e: (layer, e, 0, 0))
    idx3, aff3 = idx.reshape(B * E, 1, C), aff.reshape(B * E, 1, S)
    return pl.pallas_call(
        functools.partial(_moe_kernel, n_d=n_d),
        grid=(B, E),
        in_specs=[smem(C, 0), smem(C, -1), smem(C, 1), smem(S, 0), smem(S, -1),
                  pl.BlockSpec(memory_space=pl.ANY),
                  pl.BlockSpec((1, D), lambda b, e: (0, 0)),
                  weight((D, Fd)), weight((D, Fd)), weight((Fd, D))],
        out_specs=pl.BlockSpec(memory_space=pl.ANY),
        out_shape=jax.ShapeDtypeStruct(x_tt.shape, F32),
        scratch_shapes=[pltpu.VMEM((2, rows, LANES), F32),
                        pltpu.VMEM((2, C * n_d, LANES), F32), pltpu.VMEM((2, C * n_d, LANES), F32),
                        pltpu.VMEM((C, D), BF16),
                        pltpu.SemaphoreType.DMA, pltpu.SemaphoreType.DMA],
        compiler_params=pltpu.CompilerParams(dimension_semantics=("arbitrary", "arbitrary"),
                                             vmem_limit_bytes=MOE_VMEM_LIMIT),
        name="moe",
    )(idx3, idx3, idx3, aff3, aff3, x_tt, gf.reshape(1, D), wg, wu, wd)


def _final_norm_kernel(x_ref, g_ref, o_ref):
    _, tm, D = o_ref.shape
    xs = _row_pieces(x_ref, tm, D // LANES, True)
    rs = _inv_rms(xs, D)
    for j, xj in enumerate(xs):
        cols = slice(j * LANES, (j + 1) * LANES)
        o_ref[0, :, cols] = xj * rs * g_ref[:, cols]


def _final_norm(x_tt, g, *, S, tm):
    B = x_tt.shape[0]
    D = g.shape[0]
    n_s = S // tm
    row = pl.BlockSpec((1, tm, D), lambda i: (i // n_s, i % n_s, 0))
    return pl.pallas_call(
        _final_norm_kernel,
        grid=(B * n_s,),
        in_specs=[pl.BlockSpec((1, tm * D // LANES, LANES), lambda i: (i // n_s, i % n_s, 0)),
                  pl.BlockSpec((1, D), lambda i: (0, 0))],
        out_specs=row,
        out_shape=jax.ShapeDtypeStruct((B, S, D), F32),
        compiler_params=pltpu.CompilerParams(dimension_semantics=("arbitrary",)),
        name="final_norm",
    )(x_tt, g.reshape(1, D))


def _rope_tables(S):
    inv = 1.0 / (ROPE_THETA ** (jnp.arange(0, HEAD_DIM, 2, dtype=F32) / HEAD_DIM))
    ang = jnp.arange(S, dtype=F32)[:, None] * inv[None, :]
    cos, sin = jnp.cos(ang), jnp.sin(ang)
    reps = LANES // HEAD_DIM
    cos_t = jnp.tile(jnp.concatenate([cos, cos], axis=-1), (1, reps))
    sin_t = jnp.tile(jnp.concatenate([-sin, sin], axis=-1), (1, reps))
    return cos_t, sin_t


def kernel(x, w_in, w_out, g_attn, g_mix_a, g_mix_b, sink, g_ffn, w_router, w_gate, w_up, w_down, g_final):
    B, S, D = x.shape
    depth = w_in.shape[0]
    tm = min(512, S)
    tm_in = min(1024, S)
    C = min(CAPACITY_FACTOR * S // N_EXPERTS, S)
    cos_t, sin_t = _rope_tables(S)
    wg, wu, wd = w_gate.astype(BF16), w_up.astype(BF16), w_down.astype(BF16)
    for l in range(depth):
        tt = l > 0
        qa, ka, va, qb, kb, vb = _inproj(x, g_attn[l], w_in[l].astype(BF16), cos_t, sin_t,
                                         S=S, tm=tm_in, token_tiled=tt)
        oa = _attn_a(qa, ka, va)
        ob = _attn_b(qb, kb, vb, sink[l])
        x, aff = _outproj(oa, ob, x, g_mix_a[l], g_mix_b[l], w_out[l].astype(BF16),
                          g_ffn[l], w_router[l], S=S, tm=tm, token_tiled=tt)
        x = _moe(x, _select(aff, C), aff, g_ffn[l], wg, wu, wd, l)
    return _final_norm(x, g_final, S=S, tm=tm)
```
